```python
import jax, jax.numpy as jnp
from jax import lax
import numpy as np

D_MODEL = 2048
BATCH = 2
SEQ = 4096
DEPTH = 2
DEC_BATCH = 128
DEC_SEQ = 1
PAST_LEN = 16384
PAGE_SIZE = 128

N_A_LAYERS = DEPTH // 2
N_B_LAYERS = DEPTH - N_A_LAYERS
HEAD_DIM = 128
MIX_HEADS = 12
MEM_HEADS = 4
MEM_TOKENS = 256
Q_LORA = 512
KV_LORA = 256
NOPE_DIM = 128
ROPE_DIM = 64
V_DIM = 128
MLA_IN = Q_LORA + KV_LORA + ROPE_DIM + MEM_HEADS * HEAD_DIM
MLA_OUT = MIX_HEADS * V_DIM + MEM_HEADS * HEAD_DIM
DIL_PATTERNS = ((128, 1), (512, 4), (2048, 16))
N_DIL_GROUPS = len(DIL_PATTERNS)
DIL_HEADS_PER_GROUP = MIX_HEADS // N_DIL_GROUPS
DIL_KV_HEADS = DIL_HEADS_PER_GROUP
WIN_MAX = max(w for w, _ in DIL_PATTERNS)
DIL_IN = MIX_HEADS * HEAD_DIM + MEM_HEADS * HEAD_DIM
DIL_OUT = DIL_KV_HEADS * HEAD_DIM + MEM_HEADS * HEAD_DIM
N_EXPERT_GROUPS = 4
EXPERTS_PER_GROUP = 8
N_EXPERTS = N_EXPERT_GROUPS * EXPERTS_PER_GROUP
TOP_K_IN_GROUP = 2
EXPERT_HIDDEN = 512
ROPE_THETA = 10000.0
LN_EPS = 1e-5
RMS_EPS = 1e-6
DEEPNORM_ALPHA = (2.0 * DEPTH) ** 0.25
DEEPNORM_BETA = (8.0 * DEPTH) ** -0.25
Q_BLOCK = 128
MOE_BLOCK = 128
MLA_SCALE = (NOPE_DIM + ROPE_DIM) ** -0.5
HEAD_SCALE = HEAD_DIM ** -0.5

kernel_name = "yoco_mla_dilated_hmoe_step"


def layer_norm(x, g, b):
    xf = x.astype(jnp.float32)
    mu = jnp.mean(xf, -1, keepdims=True)
    var = jnp.mean(jnp.square(xf - mu), -1, keepdims=True)
    return ((xf - mu) * lax.rsqrt(var + LN_EPS) * g.astype(jnp.float32) + b.astype(jnp.float32)).astype(x.dtype)


def post_norm(x, delta, g, b):
    s = DEEPNORM_ALPHA * x.astype(jnp.float32) + delta.astype(jnp.float32)
    return layer_norm(s, g, b).astype(x.dtype)


def rms_norm(x, g):
    xf = x.astype(jnp.float32)
    return (xf * lax.rsqrt(jnp.mean(xf * xf, -1, keepdims=True) + RMS_EPS) * g.astype(jnp.float32)).astype(x.dtype)


def rope_tables(pos, dim):
    inv_freq = ROPE_THETA ** (-jnp.arange(0, dim, 2, dtype=jnp.float32) / dim)
    ang = pos.astype(jnp.float32)[:, None] * inv_freq[None, :]
    return jnp.cos(ang), jnp.sin(ang)


def apply_rope(x, cos, sin):
    xf = x.astype(jnp.float32)
    x1, x2 = jnp.split(xf, 2, axis=-1)
    c, s = cos[:, None, :], sin[:, None, :]
    return jnp.concatenate([x1 * c - x2 * s, x2 * c + x1 * s], axis=-1).astype(x.dtype)


def map_query_blocks(fn, *qs):
    b, t = qs[0].shape[:2]
    nb = t // Q_BLOCK
    blocks = tuple(jnp.moveaxis(q.reshape(b, nb, Q_BLOCK, *q.shape[2:]), 1, 0) for q in qs)
    out = lax.map(lambda args: fn(args[0], *args[1:]), (jnp.arange(nb, dtype=jnp.int32),) + blocks)
    out = jnp.moveaxis(out, 0, 1)
    return out.reshape(b, t, *out.shape[3:])


def mla_project(x, w_in, q_norm_g, kv_norm_g, w_q_up, cos, sin):
    b, t, _ = x.shape
    proj = jnp.einsum('btd,de->bte', x, w_in)
    c_q, c_kv, k_rope, q_mem = jnp.split(proj, [Q_LORA, Q_LORA + KV_LORA, Q_LORA + KV_LORA + ROPE_DIM], axis=-1)
    c_q = rms_norm(c_q, q_norm_g)
    q = jnp.einsum('btr,rhd->bthd', c_q, w_q_up)
    q_nope = q[..., :NOPE_DIM]
    q_rope = apply_rope(q[..., NOPE_DIM:], cos, sin)
    c_kv = rms_norm(c_kv, kv_norm_g)
    k_rope = apply_rope(k_rope[:, :, None, :], cos, sin)[:, :, 0]
    return q_nope, q_rope, c_kv, k_rope, q_mem.reshape(b, t, MEM_HEADS, HEAD_DIM)


def mla_attend_prompt(q_nope, q_rope, c_kv, k_rope, w_kv_up):
    w_uk, w_uv = w_kv_up[..., :NOPE_DIM], w_kv_up[..., NOPE_DIM:]
    k_nope = jnp.einsum('bsr,rhd->bshd', c_kv, w_uk)
    v = jnp.einsum('bsr,rhd->bshd', c_kv, w_uv)
    k_pos = jnp.arange(c_kv.shape[1], dtype=jnp.int32)

    def block(i, qn, qr):
        q_pos = i * Q_BLOCK + jnp.arange(Q_BLOCK, dtype=jnp.int32)
        s = (jnp.einsum('bqhd,bkhd->bhqk', qn, k_nope)
             + jnp.einsum('bqhd,bkd->bhqk', qr, k_rope)).astype(jnp.float32) * MLA_SCALE
        s = jnp.where(k_pos[None, :] <= q_pos[:, None], s, -jnp.inf)
        p = jax.nn.softmax(s, axis=-1)
        return jnp.einsum('bhqk,bkhd->bqhd', p.astype(v.dtype), v)

    return map_query_blocks(block, q_nope, q_rope)


def mla_attend_sample(q_nope, q_rope, c_kv_new, k_rope_new, cache_lat, cache_kr, page_table, layer, w_kv_up):
    w_uk, w_uv = w_kv_up[..., :NOPE_DIM], w_kv_up[..., NOPE_DIM:]
    q_lat = jnp.einsum('bqhd,rhd->bhqr', q_nope, w_uk).astype(jnp.float32)
    q_r = jnp.transpose(q_rope, (0, 2, 1, 3)).astype(jnp.float32)
    tn = q_nope.shape[1]

    def scores(lat, kr):
        return (jnp.einsum('bhqr,bkr->bhqk', q_lat, lat.astype(jnp.float32))
                + jnp.einsum('bhqd,bkd->bhqk', q_r, kr.astype(jnp.float32))) * MLA_SCALE

    s_new = scores(c_kv_new, k_rope_new)
    causal = jnp.arange(tn)[None, :] <= jnp.arange(tn)[:, None]
    s_new = jnp.where(causal, s_new, -jnp.inf)
    m0 = jnp.max(s_new, -1)
    p0 = jnp.exp(s_new - m0[..., None])
    den0 = jnp.sum(p0, -1)
    acc0 = jnp.einsum('bhqk,bkr->bhqr', p0, c_kv_new.astype(jnp.float32))

    def page_step(carry, phys):
        m, den, acc = carry
        lat = cache_lat[phys, :, layer]
        kr = cache_kr[phys, :, layer]
        s = scores(lat, kr)
        m_new = jnp.maximum(m, jnp.max(s, -1))
        corr = jnp.exp(m - m_new)
        p = jnp.exp(s - m_new[..., None])
        den = den * corr + jnp.sum(p, -1)
        acc = acc * corr[..., None] + jnp.einsum('bhqk,bkr->bhqr', p, lat.astype(jnp.float32))
        return (m_new, den, acc), None

    (m, den, acc), _ = lax.scan(page_step, (m0, den0, acc0), page_table.T)
    o_lat = acc / den[..., None]
    return jnp.einsum('bhqr,rhd->bqhd', o_lat, w_uv).astype(q_nope.dtype)


def mem_attend(q, k, v):
    s = jnp.einsum('bqhd,bmhd->bhqm', q, k).astype(jnp.float32) * HEAD_SCALE
    p = jax.nn.softmax(s, axis=-1)
    return jnp.einsum('bhqm,bmhd->bqhd', p.astype(v.dtype), v)


def shared_kv(x, w_kv, cos, sin):
    b, t, _ = x.shape
    kv = jnp.einsum('btd,de->bte', x, w_kv).reshape(b, t, 2, DIL_KV_HEADS, HEAD_DIM)
    return apply_rope(kv[:, :, 0], cos, sin), kv[:, :, 1]


def dil_project(x, w_in, cos, sin):
    b, t, _ = x.shape
    proj = jnp.einsum('btd,de->bte', x, w_in)
    q_dil, q_mem = jnp.split(proj, [MIX_HEADS * HEAD_DIM], axis=-1)
    q_dil = apply_rope(q_dil.reshape(b, t, MIX_HEADS, HEAD_DIM), cos, sin)
    q_dil = q_dil.reshape(b, t, N_DIL_GROUPS, DIL_HEADS_PER_GROUP, HEAD_DIM)
    return q_dil, q_mem.reshape(b, t, MEM_HEADS, HEAD_DIM)


def dilated_attend(q, k_all, v_all, q_pos, p0):
    n_rows = k_all.shape[1]
    outs, lses = [], []
    for g, (window, dilation) in enumerate(DIL_PATTERNS):
        n_keys = window // dilation + 1
        idx = q_pos[:, None] - dilation * jnp.arange(n_keys, dtype=jnp.int32)[None, :] - p0
        valid = idx >= 0
        idx = jnp.clip(idx, 0, n_rows - 1)
        k_g = jnp.take(k_all, idx, axis=1)
        v_g = jnp.take(v_all, idx, axis=1)
        s = jnp.einsum('bqhd,bqkhd->bhqk', q[:, :, g], k_g).astype(jnp.float32) * HEAD_SCALE
        s = jnp.where(valid[None, None], s, -jnp.inf)
        m = jnp.max(s, -1, keepdims=True)
        e = jnp.exp(s - m)
        den = jnp.sum(e, -1)
        o = jnp.einsum('bhqk,bqkhd->bqhd', e, v_g.astype(jnp.float32)) / jnp.transpose(den, (0, 2, 1))[..., None]
        outs.append(o)
        lses.append(m[..., 0] + jnp.log(den))
    w = jax.nn.softmax(jnp.stack(lses, 0), axis=0)
    w = jnp.transpose(w, (0, 1, 3, 2))[..., None]
    return jnp.sum(jnp.stack(outs, 0) * w, axis=0).astype(q.dtype)


def hier_moe(x, w_rg, b_rg, w_re, b_re, w_g, w_u, w_d):
    g_logits = jnp.einsum('td,dg->tg', x, w_rg).astype(jnp.float32) + b_rg.astype(jnp.float32)
    g_prob = jax.nn.softmax(g_logits, axis=-1)
    _, g_sel = lax.top_k(g_logits, 1)
    g_sel = g_sel[:, 0]
    g_w = jnp.take_along_axis(g_prob, g_sel[:, None], axis=-1)
    e_logits = (jnp.einsum('td,de->te', x, w_re).astype(jnp.float32) + b_re.astype(jnp.float32))
    e_logits = e_logits.reshape(-1, N_EXPERT_GROUPS, EXPERTS_PER_GROUP)
    e_in = jnp.take_along_axis(e_logits, g_sel[:, None, None], axis=1)[:, 0]
    e_top, e_idx = lax.top_k(e_in, TOP_K_IN_GROUP)
    e_w = jax.nn.softmax(e_top, axis=-1) * g_w
    expert_id = g_sel[:, None] * EXPERTS_PER_GROUP + e_idx
    gate = jnp.sum(jax.nn.one_hot(expert_id, N_EXPERTS, dtype=jnp.float32) * e_w[..., None], axis=1)
    h = jax.nn.silu(jnp.einsum('td,edh->teh', x, w_g)) * jnp.einsum('td,edh->teh', x, w_u)
    h = h * gate[..., None].astype(h.dtype)
    return jnp.einsum('teh,ehd->td', h, w_d)


def moe_prompt(x, *w):
    b, t, d = x.shape
    y = lax.map(lambda xt: hier_moe(xt, *w), x.reshape(-1, MOE_BLOCK, d))
    return y.reshape(b, t, d)


def setup_inputs(seed: int = 0) -> dict:
    key = jax.random.key(seed)
    keys = iter(jax.random.split(key, 40))

    def normal(shape, scale):
        return jax.random.normal(next(keys), shape, jnp.float32) * scale

    D = D_MODEL
    n_pages = PAST_LEN // PAGE_SIZE
    n_used = DEC_BATCH * n_pages
    n_phys = n_used + (n_used + 3) // 4
    wbuf = min(WIN_MAX, PAST_LEN)
    perm = jax.random.permutation(next(keys), n_phys)
    page_table = perm[:n_used].reshape(DEC_BATCH, n_pages).astype(jnp.int32)
    return {
        "x_prompt": normal((BATCH, SEQ, D), 1.0),
        "x_sample": normal((DEC_BATCH, DEC_SEQ, D), 1.0),
        "cache_mla_latent": normal((n_phys, PAGE_SIZE, N_A_LAYERS, KV_LORA), 1.0),
        "cache_mla_krope": normal((n_phys, PAGE_SIZE, N_A_LAYERS, ROPE_DIM), 1.0),
        "cache_swa_k": normal((DEC_BATCH, wbuf, DIL_KV_HEADS, HEAD_DIM), 1.0),
        "cache_swa_v": normal((DEC_BATCH, wbuf, DIL_KV_HEADS, HEAD_DIM), 1.0),
        "cache_mem_k": normal((DEC_BATCH, MEM_TOKENS, DEPTH, MEM_HEADS, HEAD_DIM), 1.0),
        "cache_mem_v": normal((DEC_BATCH, MEM_TOKENS, DEPTH, MEM_HEADS, HEAD_DIM), 1.0),
        "page_table": page_table,
        "mem_prompt": normal((BATCH, MEM_TOKENS, D), 1.0),
        "w_in_a": normal((N_A_LAYERS, D, MLA_IN), D ** -0.5),
        "q_norm_a": 1.0 + normal((N_A_LAYERS, Q_LORA), 0.02),
        "kv_norm_a": 1.0 + normal((N_A_LAYERS, KV_LORA), 0.02),
        "w_q_up_a": normal((N_A_LAYERS, Q_LORA, MIX_HEADS, NOPE_DIM + ROPE_DIM), Q_LORA ** -0.5),
        "w_kv_up_a": normal((N_A_LAYERS, KV_LORA, MIX_HEADS, NOPE_DIM + V_DIM), KV_LORA ** -0.5),
        "w_out_a": normal((N_A_LAYERS, MLA_OUT, D), DEEPNORM_BETA * MLA_OUT ** -0.5),
        "w_in_b": normal((N_B_LAYERS, D, DIL_IN), D ** -0.5),
        "w_kv_shared": normal((D, 2 * DIL_KV_HEADS * HEAD_DIM), D ** -0.5),
        "w_out_b": normal((N_B_LAYERS, DIL_OUT, D), DEEPNORM_BETA * DIL_OUT ** -0.5),
        "w_mem_kv": normal((DEPTH, D, 2 * MEM_HEADS * HEAD_DIM), D ** -0.5),
        "ln1_g": 1.0 + normal((DEPTH, D), 0.02),
        "ln1_b": normal((DEPTH, D), 0.02),
        "ln2_g": 1.0 + normal((DEPTH, D), 0.02),
        "ln2_b": normal((DEPTH, D), 0.02),
        "w_router_group": normal((DEPTH, D, N_EXPERT_GROUPS), D ** -0.5),
        "b_router_group": normal((DEPTH, N_EXPERT_GROUPS), 0.01),
        "w_router_expert": normal((DEPTH, D, N_EXPERTS), D ** -0.5),
        "b_router_expert": normal((DEPTH, N_EXPERTS), 0.01),
        "w_exp_gate": normal((DEPTH, N_EXPERTS, D, EXPERT_HIDDEN), D ** -0.5),
        "w_exp_up": normal((DEPTH, N_EXPERTS, D, EXPERT_HIDDEN), D ** -0.5),
        "w_exp_down": normal((DEPTH, N_EXPERTS, EXPERT_HIDDEN, D), DEEPNORM_BETA * EXPERT_HIDDEN ** -0.5),
    }


def reference(x_prompt, x_sample, cache_mla_latent, cache_mla_krope, cache_swa_k, cache_swa_v,
              cache_mem_k, cache_mem_v, page_table, mem_prompt,
              w_in_a, q_norm_a, kv_norm_a, w_q_up_a, w_kv_up_a, w_out_a,
              w_in_b, w_kv_shared, w_out_b, w_mem_kv,
              ln1_g, ln1_b, ln2_g, ln2_b,
              w_router_group, b_router_group, w_router_expert, b_router_expert,
              w_exp_gate, w_exp_up, w_exp_down):
    bp, tp, _ = x_prompt.shape
    bs, ts, _ = x_sample.shape
    wbuf = cache_swa_k.shape[1]
    pos_p = jnp.arange(tp, dtype=jnp.int32)
    pos_s = PAST_LEN + jnp.arange(ts, dtype=jnp.int32)
    cos_r_p, sin_r_p = rope_tables(pos_p, ROPE_DIM)
    cos_r_s, sin_r_s = rope_tables(pos_s, ROPE_DIM)
    cos_h_p, sin_h_p = rope_tables(pos_p, HEAD_DIM)
    cos_h_s, sin_h_s = rope_tables(pos_s, HEAD_DIM)

    hp, hs = x_prompt, x_sample
    lat_p, kr_p, lat_s, kr_s, memk_p, memv_p = [], [], [], [], [], []
    for l in range(DEPTH):
        mkv = jnp.einsum('bmd,de->bme', mem_prompt, w_mem_kv[l]).reshape(bp, MEM_TOKENS, 2, MEM_HEADS, HEAD_DIM)
        mk_p, mv_p = mkv[:, :, 0], mkv[:, :, 1]
        memk_p.append(mk_p)
        memv_p.append(mv_p)
        mk_s, mv_s = cache_mem_k[:, :, l], cache_mem_v[:, :, l]

        if l < N_A_LAYERS:
            a = l
            qn_p, qr_p, ckv_p, kro_p, qm_p = mla_project(hp, w_in_a[a], q_norm_a[a], kv_norm_a[a], w_q_up_a[a], cos_r_p, sin_r_p)
            qn_s, qr_s, ckv_s, kro_s, qm_s = mla_project(hs, w_in_a[a], q_norm_a[a], kv_norm_a[a], w_q_up_a[a], cos_r_s, sin_r_s)
            lat_p.append(ckv_p)
            kr_p.append(kro_p)
            lat_s.append(ckv_s)
            kr_s.append(kro_s)
            mix_p = mla_attend_prompt(qn_p, qr_p, ckv_p, kro_p, w_kv_up_a[a])
            mix_s = mla_attend_sample(qn_s, qr_s, ckv_s, kro_s, cache_mla_latent, cache_mla_krope, page_table, a, w_kv_up_a[a])
            w_out = w_out_a[a]
        else:
            b = l - N_A_LAYERS
            if b == 0:
                k_p, v_p = shared_kv(hp, w_kv_shared, cos_h_p, sin_h_p)
                k_new_s, v_new_s = shared_kv(hs, w_kv_shared, cos_h_s, sin_h_s)
                k_all_s = jnp.concatenate([cache_swa_k, k_new_s], axis=1)
                v_all_s = jnp.concatenate([cache_swa_v, v_new_s], axis=1)
            qd_p, qm_p = dil_project(hp, w_in_b[b], cos_h_p, sin_h_p)
            qd_s, qm_s = dil_project(hs, w_in_b[b], cos_h_s, sin_h_s)
            mix_p = map_query_blocks(
                lambda i, q: dilated_attend(q, k_p, v_p, i * Q_BLOCK + jnp.arange(Q_BLOCK, dtype=jnp.int32), 0), qd_p)
            mix_s = dilated_attend(qd_s, k_all_s, v_all_s, pos_s, PAST_LEN - wbuf)
            w_out = w_out_b[b]

        att_p = jnp.concatenate([mix_p.reshape(bp, tp, -1), mem_attend(qm_p, mk_p, mv_p).reshape(bp, tp, -1)], axis=-1)
        att_s = jnp.concatenate([mix_s.reshape(bs, ts, -1), mem_attend(qm_s, mk_s, mv_s).reshape(bs, ts, -1)], axis=-1)
        hp = post_norm(hp, jnp.einsum('bte,ed->btd', att_p, w_out), ln1_g[l], ln1_b[l])
        hs = post_norm(hs, jnp.einsum('bte,ed->btd', att_s, w_out), ln1_g[l], ln1_b[l])

        moe_w = (w_router_group[l], b_router_group[l], w_router_expert[l], b_router_expert[l],
                 w_exp_gate[l], w_exp_up[l], w_exp_down[l])
        hp = post_norm(hp, moe_prompt(hp, *moe_w), ln2_g[l], ln2_b[l])
        hs = post_norm(hs, hier_moe(hs.reshape(-1, D_MODEL), *moe_w).reshape(bs, ts, D_MODEL), ln2_g[l], ln2_b[l])

    y_prompt, y_sample = hp, hs
    new_mla_latent_prompt = jnp.stack(lat_p, axis=2)
    new_mla_krope_prompt = jnp.stack(kr_p, axis=2)
    new_mla_latent_sample = jnp.stack(lat_s, axis=2)
    new_mla_krope_sample = jnp.stack(kr_s, axis=2)
    keep_p = min(WIN_MAX, tp)
    new_swa_k_prompt = k_p[:, tp - keep_p:]
    new_swa_v_prompt = v_p[:, tp - keep_p:]
    new_swa_k_sample = k_all_s[:, ts:]
    new_swa_v_sample = v_all_s[:, ts:]
    new_mem_k_prompt = jnp.stack(memk_p, axis=2)
    new_mem_v_prompt = jnp.stack(memv_p, axis=2)
    return (y_prompt, y_sample, new_mla_latent_prompt, new_mla_krope_prompt, new_mla_latent_sample, new_mla_krope_sample, new_swa_k_prompt, new_swa_v_prompt, new_swa_k_sample, new_swa_v_sample, new_mem_k_prompt, new_mem_v_prompt)
```

```python
import functools

import jax
import jax.numpy as jnp
import numpy as np
from jax import lax
from jax.experimental import pallas as pl
from jax.experimental.pallas import tpu as pltpu

BF16 = jnp.bfloat16
F32 = jnp.float32

D_MODEL = 2048
DEPTH = 2
HEAD_DIM = 128
MIX_HEADS = 12
MEM_HEADS = 4
MEM_DIM = MEM_HEADS * HEAD_DIM
Q_LORA = 512
KV_LORA = 256
NOPE_DIM = 128
ROPE_DIM = 64
V_DIM = 128
PAGE_SIZE = 128
DIL_PATTERNS = ((128, 1), (512, 4), (2048, 16))
N_DIL_GROUPS = len(DIL_PATTERNS)
DIL_KV_HEADS = MIX_HEADS // N_DIL_GROUPS
DIL_KV_DIM = DIL_KV_HEADS * HEAD_DIM
DIL_KEYS = 128
N_EXPERT_GROUPS = 4
EXPERTS_PER_GROUP = 8
N_EXPERTS = N_EXPERT_GROUPS * EXPERTS_PER_GROUP
EXPERT_HIDDEN = 512
ROPE_THETA = 10000.0
LN_EPS = 1e-5
RMS_EPS = 1e-6
DEEPNORM_ALPHA = (2.0 * DEPTH) ** 0.25
MLA_SCALE = (NOPE_DIM + ROPE_DIM) ** -0.5
HEAD_SCALE = HEAD_DIM ** -0.5

LANES = 128
MLA_QK = 2 * LANES
VMEM_LIMIT = 56 * 1024 * 1024
MOE_TM = 256
MLA_PAGES_PER_STEP = 16
NEG_INF = float("-inf")


def _cparams(n_grid):
    return pltpu.CompilerParams(dimension_semantics=("arbitrary",) * n_grid, vmem_limit_bytes=VMEM_LIMIT)


def _row_tile(m):
    for t in (512, 256, 128, 64, 32, 16, 8):
        if m % t == 0:
            return t
    raise ValueError(f"row count {m} not a multiple of 8")


def _dot(a, b):
    return jnp.dot(a, b, preferred_element_type=F32)


def _dot_t(a, b):
    return lax.dot_general(a, b, (((1,), (1,)), ((), ())), preferred_element_type=F32)


def _rope_tables(pos, dim, width):
    half = dim // 2
    inv_freq = ROPE_THETA ** (-jnp.arange(0, dim, 2, dtype=F32) / dim)
    ang = pos.astype(F32)[:, None] * inv_freq[None, :]
    cos, sin = jnp.cos(ang), jnp.sin(ang)
    z = jnp.zeros((pos.shape[0], width - dim), F32)
    zh = jnp.zeros_like(cos)
    c = jnp.concatenate([cos, cos, z], axis=-1)
    s_lo = jnp.concatenate([-sin, zh, z], axis=-1)
    s_hi = jnp.concatenate([zh, sin, z], axis=-1)
    return c, s_lo, s_hi


def _rope_apply(x, c, s_lo, s_hi, half):
    width = x.shape[-1]
    return x * c + pltpu.roll(x, width - half, 1) * s_lo + pltpu.roll(x, half, 1) * s_hi


def _mm_kernel(x_ref, w_ref, o_ref):
    o_ref[...] = _dot(x_ref[...].astype(BF16), w_ref[...]).astype(o_ref.dtype)


def _matmul(x, w, out_dtype=F32):
    m, k = x.shape
    n = w.shape[1]
    tm = _row_tile(m)
    return pl.pallas_call(
        _mm_kernel,
        grid=(m // tm,),
        in_specs=[pl.BlockSpec((tm, k), lambda i: (i, 0)), pl.BlockSpec((k, n), lambda i: (0, 0))],
        out_specs=pl.BlockSpec((tm, n), lambda i: (i, 0)),
        out_shape=jax.ShapeDtypeStruct((m, n), out_dtype),
        compiler_params=_cparams(1),
        name="matmul",
    )(x, w)


def _bmm_kernel(x_ref, w_ref, o_ref):
    o_ref[0] = _dot(x_ref[0].astype(BF16), w_ref[0]).astype(o_ref.dtype)


def _head_matmul(x, w, out_dtype):
    h, m, k = x.shape
    n = w.shape[2]
    return pl.pallas_call(
        _bmm_kernel,
        grid=(h,),
        in_specs=[pl.BlockSpec((1, m, k), lambda i: (i, 0, 0)), pl.BlockSpec((1, k, n), lambda i: (i, 0, 0))],
        out_specs=pl.BlockSpec((1, m, n), lambda i: (i, 0, 0)),
        out_shape=jax.ShapeDtypeStruct((h, m, n), out_dtype),
        compiler_params=_cparams(1),
        name="head_matmul",
    )(x, w)


def _mla_in_kernel(x_ref, w_ref, qg_ref, kvg_ref, c_ref, slo_ref, shi_ref, cq_ref, lat_ref, kr_ref, qm_ref):
    xb = x_ref[...].astype(BF16)
    c_q = _dot(xb, w_ref[:, 0:Q_LORA])
    cq_ref[...] = (c_q * lax.rsqrt(jnp.mean(c_q * c_q, -1, keepdims=True) + RMS_EPS) * qg_ref[...]).astype(cq_ref.dtype)
    o = Q_LORA
    c_kv = _dot(xb, w_ref[:, o:o + KV_LORA])
    lat_ref[...] = c_kv * lax.rsqrt(jnp.mean(c_kv * c_kv, -1, keepdims=True) + RMS_EPS) * kvg_ref[...]
    o += KV_LORA
    k_rope = _dot(xb, w_ref[:, o:o + LANES])
    kr_ref[...] = _rope_apply(k_rope, c_ref[...], slo_ref[...], shi_ref[...], ROPE_DIM // 2)
    o += LANES
    qm_ref[...] = (_dot(xb, w_ref[:, o:o + MEM_DIM]) * HEAD_SCALE).astype(qm_ref.dtype)


def _mla_in_proj(x, w, q_g, kv_g, tabs, n_tab_blocks):
    m, d = x.shape
    tm = _row_tile(m)
    n = w.shape[1]
    row = lambda i: (i, 0)
    fixed = lambda i: (0, 0)
    tab = lambda i: (i % n_tab_blocks, 0)
    return pl.pallas_call(
        _mla_in_kernel,
        grid=(m // tm,),
        in_specs=[pl.BlockSpec((tm, d), row), pl.BlockSpec((d, n), fixed),
                  pl.BlockSpec((1, Q_LORA), fixed), pl.BlockSpec((1, KV_LORA), fixed),
                  pl.BlockSpec((tm, LANES), tab), pl.BlockSpec((tm, LANES), tab), pl.BlockSpec((tm, LANES), tab)],
        out_specs=[pl.BlockSpec((tm, Q_LORA), row), pl.BlockSpec((tm, KV_LORA), row),
                   pl.BlockSpec((tm, LANES), row), pl.BlockSpec((tm, MEM_DIM), row)],
        out_shape=[jax.ShapeDtypeStruct((m, Q_LORA), BF16), jax.ShapeDtypeStruct((m, KV_LORA), F32),
                   jax.ShapeDtypeStruct((m, LANES), F32), jax.ShapeDtypeStruct((m, MEM_DIM), BF16)],
        compiler_params=_cparams(1),
        name="mla_in_proj",
    )(x, w, q_g, kv_g, *tabs)


def _q_up_kernel(cq_ref, w_ref, c_ref, slo_ref, shi_ref, q_ref):
    cq = cq_ref[...]
    c, slo, shi = c_ref[...], slo_ref[...], shi_ref[...]
    for h in range(MIX_HEADS):
        r = _dot(cq, w_ref[:, h * MLA_QK:(h + 1) * MLA_QK])
        q_ref[:, h * MLA_QK:h * MLA_QK + LANES] = (r[:, :LANES] * MLA_SCALE).astype(q_ref.dtype)
        roped = _rope_apply(r[:, LANES:], c, slo, shi, ROPE_DIM // 2)
        q_ref[:, h * MLA_QK + LANES:(h + 1) * MLA_QK] = (roped * MLA_SCALE).astype(q_ref.dtype)


def _q_up(cq, w, tabs, n_tab_blocks):
    m, k = cq.shape
    tm = _row_tile(m)
    n = w.shape[1]
    row = lambda i: (i, 0)
    fixed = lambda i: (0, 0)
    tab = lambda i: (i % n_tab_blocks, 0)
    return pl.pallas_call(
        _q_up_kernel,
        grid=(m // tm,),
        in_specs=[pl.BlockSpec((tm, k), row), pl.BlockSpec((k, n), fixed),
                  pl.BlockSpec((tm, LANES), tab), pl.BlockSpec((tm, LANES), tab), pl.BlockSpec((tm, LANES), tab)],
        out_specs=pl.BlockSpec((tm, n), row),
        out_shape=jax.ShapeDtypeStruct((m, n), BF16),
        compiler_params=_cparams(1),
        name="mla_q_up",
    )(cq, w, *tabs)


def _kv_up_kernel(lat_ref, kr_ref, w_ref, k_ref, v_ref):
    lat = lat_ref[...].astype(BF16)
    kr = kr_ref[...].astype(BF16)
    for h in range(MIX_HEADS):
        r = _dot(lat, w_ref[:, h * 2 * LANES:(h + 1) * 2 * LANES])
        k_ref[:, h * MLA_QK:h * MLA_QK + LANES] = r[:, :LANES].astype(BF16)
        k_ref[:, h * MLA_QK + LANES:(h + 1) * MLA_QK] = kr
        v_ref[:, h * V_DIM:(h + 1) * V_DIM] = r[:, LANES:].astype(BF16)


def _kv_up(lat, kr, w):
    m = lat.shape[0]
    tm = _row_tile(m)
    row = lambda i: (i, 0)
    return pl.pallas_call(
        _kv_up_kernel,
        grid=(m // tm,),
        in_specs=[pl.BlockSpec((tm, KV_LORA), row), pl.BlockSpec((tm, LANES), row),
                  pl.BlockSpec(w.shape, lambda i: (0, 0))],
        out_specs=[pl.BlockSpec((tm, MIX_HEADS * MLA_QK), row), pl.BlockSpec((tm, MIX_HEADS * V_DIM), row)],
        out_shape=[jax.ShapeDtypeStruct((m, MIX_HEADS * MLA_QK), BF16),
                   jax.ShapeDtypeStruct((m, MIX_HEADS * V_DIM), BF16)],
        compiler_params=_cparams(1),
        name="mla_kv_up",
    )(lat, kr, w)


def _mla_flash_kernel(q_ref, k_ref, v_ref, o_ref, *, tq):
    i = pl.program_id(2)
    q = q_ref[0]

    def step(j, carry, masked):
        m, l, acc = carry
        start = pl.multiple_of(j * tq, tq)
        s = _dot_t(q, k_ref[0, pl.ds(start, tq), :])
        if masked:
            rows = lax.broadcasted_iota(jnp.int32, s.shape, 0)
            cols = lax.broadcasted_iota(jnp.int32, s.shape, 1)
            s = jnp.where(cols <= rows, s, NEG_INF)
        m_new = jnp.maximum(m, jnp.max(s, -1, keepdims=True))
        corr = jnp.exp(m - m_new)
        p = jnp.exp(s - m_new)
        l = l * corr + jnp.sum(p, -1, keepdims=True)
        acc = acc * corr + _dot(p.astype(BF16), v_ref[0, pl.ds(start, tq), :])
        return m_new, l, acc

    init = (jnp.full((tq, 1), NEG_INF, F32), jnp.zeros((tq, 1), F32), jnp.zeros((tq, V_DIM), F32))
    carry = lax.fori_loop(0, i, lambda j, c: step(j, c, False), init)
    _, l, acc = step(i, carry, True)
    o_ref[0] = (acc / l).astype(o_ref.dtype)


def _mla_flash(q, k, v):
    b, t, _ = q.shape
    tq = _row_tile(t)
    return pl.pallas_call(
        functools.partial(_mla_flash_kernel, tq=tq),
        grid=(b, MIX_HEADS, t // tq),
        in_specs=[pl.BlockSpec((1, tq, MLA_QK), lambda b_, h, i: (b_, i, h)),
                  pl.BlockSpec((1, t, MLA_QK), lambda b_, h, i: (b_, 0, h)),
                  pl.BlockSpec((1, t, V_DIM), lambda b_, h, i: (b_, 0, h))],
        out_specs=pl.BlockSpec((1, tq, V_DIM), lambda b_, h, i: (b_, i, h)),
        out_shape=jax.ShapeDtypeStruct((b, t, MIX_HEADS * V_DIM), BF16),
        compiler_params=_cparams(3),
        name="mla_flash",
    )(q, k, v)


def _mem_attn_kernel(q_ref, k_ref, v_ref, o_ref):
    for h in range(MEM_HEADS):
        sl = slice(h * HEAD_DIM, (h + 1) * HEAD_DIM)
        s = _dot_t(q_ref[0, :, sl], k_ref[0, :, sl].astype(BF16))
        p = jnp.exp(s - jnp.max(s, -1, keepdims=True))
        o = _dot(p.astype(BF16), v_ref[0, :, sl].astype(BF16))
        o_ref[0, :, sl] = (o / jnp.sum(p, -1, keepdims=True)).astype(o_ref.dtype)


def _mem_attn_prompt(q, mkv):
    b, t, _ = q.shape
    n_mem = mkv.shape[1]
    tq = _row_tile(t)
    return pl.pallas_call(
        _mem_attn_kernel,
        grid=(b, t // tq),
        in_specs=[pl.BlockSpec((1, tq, MEM_DIM), lambda b_, i: (b_, i, 0)),
                  pl.BlockSpec((1, n_mem, MEM_DIM), lambda b_, i: (b_, 0, 0)),
                  pl.BlockSpec((1, n_mem, MEM_DIM), lambda b_, i: (b_, 0, 1))],
        out_specs=pl.BlockSpec((1, tq, MEM_DIM), lambda b_, i: (b_, i, 0)),
        out_shape=jax.ShapeDtypeStruct((b, t, MEM_DIM), BF16),
        compiler_params=_cparams(2),
        name="mem_attn_prompt",
    )(q, mkv, mkv)


def _head_rows(width, rows=8):
    r = lax.broadcasted_iota(jnp.int32, (rows, width), 0)
    c = lax.broadcasted_iota(jnp.int32, (rows, width), 1)
    return (c // HEAD_DIM) == r


def _single_query_attend(q_row, k_heads, v_heads, valid_rows):
    n = k_heads[0].shape[0]
    row_id = lax.broadcasted_iota(jnp.int32, (8, HEAD_DIM), 0)
    s = None
    for h, k_h in enumerate(k_heads):
        q_h = jnp.broadcast_to(q_row[:, h * HEAD_DIM:(h + 1) * HEAD_DIM], (8, HEAD_DIM))
        s_h = _dot_t(k_h, jnp.where(row_id == h, q_h, 0.0).astype(BF16))
        s = s_h if s is None else s + s_h
    if valid_rows < n:
        s = jnp.where(lax.broadcasted_iota(jnp.int32, s.shape, 0) < valid_rows, s, NEG_INF)
    m = jnp.max(s, 0, keepdims=True)
    e = jnp.exp(s - m)
    l = jnp.sum(e, 0, keepdims=True)
    eb = e.astype(BF16)
    o = jnp.concatenate(
        [lax.dot_general(eb, v_h, (((0,), (0,)), ((), ())), preferred_element_type=F32) for v_h in v_heads], axis=-1)
    eye = lax.broadcasted_iota(jnp.int32, (8, 8), 0) == lax.broadcasted_iota(jnp.int32, (8, 8), 1)
    lse = jnp.sum(jnp.where(eye, jnp.broadcast_to(m + jnp.log(l), (8, 8)), 0.0), -1, keepdims=True)
    l_col = jnp.sum(jnp.where(eye, jnp.broadcast_to(l, (8, 8)), 0.0), -1, keepdims=True)
    return o / l_col, lse


def _collapse_heads(o):
    return jnp.sum(jnp.where(_head_rows(o.shape[1]), o, 0.0), 0, keepdims=True)


def _mem_attn_sample_kernel(q_ref, k_ref, v_ref, o_ref, *, bb):
    n_mem = k_ref.shape[1]
    for j in range(bb):
        k = [k_ref[j, :, h, :].astype(BF16) for h in range(MEM_HEADS)]
        v = [v_ref[j, :, h, :].astype(BF16) for h in range(MEM_HEADS)]
        o, _ = _single_query_attend(q_ref[pl.ds(j, 1), :].astype(F32), k, v, n_mem)
        o_ref[pl.ds(j, 1), :] = _collapse_heads(o).astype(o_ref.dtype)


def _mem_attn_sample(q, cache_k, cache_v, layer):
    b = q.shape[0]
    n_mem = cache_k.shape[1]
    bb = 8
    cspec = pl.BlockSpec((bb, n_mem, None, MEM_HEADS, HEAD_DIM), lambda i: (i, 0, layer, 0, 0))
    return pl.pallas_call(
        functools.partial(_mem_attn_sample_kernel, bb=bb),
        grid=(b // bb,),
        in_specs=[pl.BlockSpec((bb, MEM_DIM), lambda i: (i, 0)), cspec, cspec],
        out_specs=pl.BlockSpec((bb, MEM_DIM), lambda i: (i, 0)),
        out_shape=jax.ShapeDtypeStruct((b, MEM_DIM), BF16),
        compiler_params=_cparams(1),
        name="mem_attn_sample",
    )(q, cache_k, cache_v)


def _mla_decode_kernel(pt_ref, qlat_ref, qr_ref, latn_ref, krn_ref, *refs, pages):
    del pt_ref
    lat_refs, kr_refs = refs[:pages], refs[pages:2 * pages]
    o_ref, m_sc, l_sc, acc_sc = refs[2 * pages:]
    c = pl.program_id(1)
    qlat = qlat_ref[0]
    qr_full = qr_ref[0]
    qr = qr_full[:, :ROPE_DIM]

    @pl.when(c == 0)
    def _():
        s0 = (jnp.sum(qlat.astype(F32) * latn_ref[0], -1, keepdims=True)
              + jnp.sum(qr_full.astype(F32) * krn_ref[0], -1, keepdims=True))
        m_sc[...] = s0
        l_sc[...] = jnp.ones_like(s0)
        acc_sc[...] = jnp.broadcast_to(latn_ref[0], acc_sc.shape)

    scores = []
    for j in range(pages):
        lat_b = lat_refs[j][:, 0, :].astype(BF16)
        kr_t = kr_refs[j][...].astype(BF16)
        scores.append(_dot_t(qlat, lat_b) + _dot(qr, kr_t))
    m_old = m_sc[...]
    tile_max = functools.reduce(jnp.maximum, scores)
    m_new = jnp.maximum(m_old, jnp.max(tile_max, -1, keepdims=True))
    corr = jnp.exp(m_old - m_new)
    l = l_sc[...] * corr
    acc = acc_sc[...] * corr
    for j in range(pages):
        p = jnp.exp(scores[j] - m_new)
        l = l + jnp.sum(p, -1, keepdims=True)
        acc = acc + _dot(p.astype(BF16), lat_refs[j][:, 0, :].astype(BF16))
    m_sc[...] = m_new
    l_sc[...] = l
    acc_sc[...] = acc

    @pl.when(c == pl.num_programs(1) - 1)
    def _():
        o_ref[0] = acc / l


def _mla_decode(qlat, qr, lat_new, kr_new, cache_lat, cache_kr, page_table, layer):
    b, hp, _ = qlat.shape
    n_pages = page_table.shape[1]
    pages = min(MLA_PAGES_PER_STEP, n_pages)
    assert n_pages % pages == 0
    per_b = lambda i, c, pt: (i, 0, 0)

    def page_map(j):
        return lambda i, c, pt: (pt[i * n_pages + c * pages + j], 0, layer, 0)

    lat_specs = [pl.BlockSpec((None, PAGE_SIZE, 1, KV_LORA), page_map(j)) for j in range(pages)]
    cache_kr = jnp.transpose(cache_kr, (0, 2, 3, 1))
    kr_specs = [pl.BlockSpec((None, None, ROPE_DIM, PAGE_SIZE),
                             (lambda j_: lambda i, c, pt: (pt[i * n_pages + c * pages + j_], layer, 0, 0))(j))
                for j in range(pages)]
    grid_spec = pltpu.PrefetchScalarGridSpec(
        num_scalar_prefetch=1,
        grid=(b, n_pages // pages),
        in_specs=[pl.BlockSpec((1, hp, KV_LORA), per_b), pl.BlockSpec((1, hp, LANES), per_b),
                  pl.BlockSpec((1, 1, KV_LORA), per_b), pl.BlockSpec((1, 1, LANES), per_b)] + lat_specs + kr_specs,
        out_specs=pl.BlockSpec((1, hp, KV_LORA), per_b),
        scratch_shapes=[pltpu.VMEM((hp, 1), F32), pltpu.VMEM((hp, 1), F32), pltpu.VMEM((hp, KV_LORA), F32)],
    )
    return pl.pallas_call(
        functools.partial(_mla_decode_kernel, pages=pages),
        grid_spec=grid_spec,
        out_shape=jax.ShapeDtypeStruct((b, hp, KV_LORA), F32),
        compiler_params=_cparams(2),
        name="mla_decode",
    )(page_table.reshape(-1), qlat, qr, lat_new, kr_new, *([cache_lat] * pages), *([cache_kr] * pages))


def _proj_rope_kernel(x_ref, w_ref, c_ref, slo_ref, shi_ref, *out_refs, segs):
    xb = x_ref[...].astype(BF16)
    for c0, n, rope, scale, oi, o0 in segs:
        r = _dot(xb, w_ref[:, c0:c0 + n])
        if rope:
            c, slo, shi = c_ref[...], slo_ref[...], shi_ref[...]
            r = jnp.concatenate([_rope_apply(r[:, a:a + HEAD_DIM], c, slo, shi, HEAD_DIM // 2)
                                 for a in range(0, n, HEAD_DIM)], axis=-1)
        if scale != 1.0:
            r = r * scale
        out_refs[oi][:, o0:o0 + n] = r.astype(out_refs[oi].dtype)


def _proj_rope(x, w, tabs, n_tab_blocks, segs, outs):
    m, d = x.shape
    tm = _row_tile(m)
    row = lambda i: (i, 0)
    tab = lambda i: (i % n_tab_blocks, 0)
    return pl.pallas_call(
        functools.partial(_proj_rope_kernel, segs=segs),
        grid=(m // tm,),
        in_specs=[pl.BlockSpec((tm, d), row), pl.BlockSpec(w.shape, lambda i: (0, 0)),
                  pl.BlockSpec((tm, LANES), tab), pl.BlockSpec((tm, LANES), tab), pl.BlockSpec((tm, LANES), tab)],
        out_specs=[pl.BlockSpec((tm, n), row) for n, _ in outs],
        out_shape=[jax.ShapeDtypeStruct((m, n), dt) for n, dt in outs],
        compiler_params=_cparams(1),
        name="proj_rope",
    )(x, w, *tabs)


def _band_attn_kernel(q_ref, kp_ref, kc_ref, vp_ref, vc_ref, o_ref, lse_ref):
    i = pl.program_id(1)
    t = q_ref.shape[1]
    rows = lax.broadcasted_iota(jnp.int32, (t, t), 0)
    cols = lax.broadcasted_iota(jnp.int32, (t, t), 1)
    mask_cur = cols <= rows
    mask_prev = jnp.logical_and(cols >= rows, i > 0)
    for h in range(DIL_KV_HEADS):
        sl = slice(h * HEAD_DIM, (h + 1) * HEAD_DIM)
        q = q_ref[0, :, sl]
        s_c = jnp.where(mask_cur, _dot_t(q, kc_ref[0, :, sl]), NEG_INF)
        s_p = jnp.where(mask_prev, _dot_t(q, kp_ref[0, :, sl]), NEG_INF)
        m = jnp.maximum(jnp.max(s_c, -1, keepdims=True), jnp.max(s_p, -1, keepdims=True))
        e_c = jnp.exp(s_c - m)
        e_p = jnp.exp(s_p - m)
        l = jnp.sum(e_c, -1, keepdims=True) + jnp.sum(e_p, -1, keepdims=True)
        o = _dot(e_c.astype(BF16), vc_ref[0, :, sl]) + _dot(e_p.astype(BF16), vp_ref[0, :, sl])
        o_ref[0, :, sl] = o / l
        lse_ref[0, :, sl] = jnp.broadcast_to(m + jnp.log(l), (t, HEAD_DIM))


def _band_attn(q, k, v):
    s, l, w = q.shape
    t = DIL_KEYS
    cur = lambda a, i: (a, i, 0)
    prev = lambda a, i: (a, jnp.maximum(i - 1, 0), 0)
    blk = (1, t, w)
    return pl.pallas_call(
        _band_attn_kernel,
        grid=(s, l // t),
        in_specs=[pl.BlockSpec(blk, cur), pl.BlockSpec(blk, prev), pl.BlockSpec(blk, cur),
                  pl.BlockSpec(blk, prev), pl.BlockSpec(blk, cur)],
        out_specs=[pl.BlockSpec(blk, cur), pl.BlockSpec(blk, cur)],
        out_shape=[jax.ShapeDtypeStruct((s, l, w), F32), jax.ShapeDtypeStruct((s, l, w), F32)],
        compiler_params=_cparams(2),
        name="band_attn",
    )(q, k, k, v, v)


def _dil_sample_kernel(q_ref, kn_ref, vn_ref, *refs, bb):
    k_refs, v_refs, o_ref = refs[:N_DIL_GROUPS], refs[N_DIL_GROUPS:2 * N_DIL_GROUPS], refs[2 * N_DIL_GROUPS]
    for j in range(bb):
        outs, lses = [], []
        for g in range(N_DIL_GROUPS):
            q_row = q_ref[pl.ds(j, 1), g * DIL_KV_DIM:(g + 1) * DIL_KV_DIM]
            k_heads, v_heads = [], []
            for h in range(DIL_KV_HEADS):
                sl = slice(h * HEAD_DIM, (h + 1) * HEAD_DIM)
                k_new = jnp.broadcast_to(kn_ref[pl.ds(j, 1), sl], (8, HEAD_DIM))
                v_new = jnp.broadcast_to(vn_ref[pl.ds(j, 1), sl], (8, HEAD_DIM))
                k_heads.append(jnp.concatenate([k_refs[g][j, :, h, :], k_new], 0).astype(BF16))
                v_heads.append(jnp.concatenate([v_refs[g][j, :, h, :], v_new], 0).astype(BF16))
            o, lse = _single_query_attend(q_row, k_heads, v_heads, DIL_KEYS + 1)
            outs.append(o)
            lses.append(lse)
        mx = functools.reduce(jnp.maximum, lses)
        ws = [jnp.exp(x - mx) for x in lses]
        mix = sum(o * w for o, w in zip(outs, ws)) / sum(ws)
        o_ref[pl.ds(j, 1), :] = _collapse_heads(mix).astype(o_ref.dtype)


def _dil_sample(q, k_new, v_new, cache_k, cache_v):
    b, wbuf = cache_k.shape[:2]
    bb = 8
    views, specs = [], []
    for window, d in DIL_PATTERNS:
        assert wbuf % (d * DIL_KEYS) == 0 and wbuf >= window
        last = wbuf // d // DIL_KEYS - 1
        specs.append(pl.BlockSpec((bb, DIL_KEYS, None, DIL_KV_HEADS, HEAD_DIM),
                                  functools.partial(lambda i, last_: (i, last_, 0, 0, 0), last_=last)))
        views.append(lambda c, d_=d: c.reshape(b, wbuf // d_, d_, DIL_KV_HEADS, HEAD_DIM))
    row = lambda i: (i, 0)
    return pl.pallas_call(
        functools.partial(_dil_sample_kernel, bb=bb),
        grid=(b // bb,),
        in_specs=[pl.BlockSpec((bb, MIX_HEADS * HEAD_DIM), row), pl.BlockSpec((bb, DIL_KV_DIM), row),
                  pl.BlockSpec((bb, DIL_KV_DIM), row)] + specs + specs,
        out_specs=pl.BlockSpec((bb, DIL_KV_DIM), row),
        out_shape=jax.ShapeDtypeStruct((b, DIL_KV_DIM), BF16),
        compiler_params=_cparams(1),
        name="dil_sample",
    )(q, k_new, v_new, *[f(cache_k) for f in views], *[f(cache_v) for f in views])


def _swa_shift_kernel(ck_ref, cv_ref, kn_ref, vn_ref, ok_ref, ov_ref, sem, *, n_chunks):
    b, w = ck_ref.shape[:2]
    bc = b // n_chunks
    copies = []
    for src, new, dst in ((ck_ref, kn_ref, ok_ref), (cv_ref, vn_ref, ov_ref)):
        for c in range(n_chunks):
            copies.append(pltpu.make_async_copy(src.at[pl.ds(c * bc, bc), pl.ds(1, w - 1)],
                                                dst.at[pl.ds(c * bc, bc), pl.ds(0, w - 1)], sem.at[len(copies)]))
        copies.append(pltpu.make_async_copy(new, dst.at[:, pl.ds(w - 1, 1)], sem.at[len(copies)]))
    for cp in copies:
        cp.start()
    for cp in copies:
        cp.wait()


def _swa_shift(cache_k, cache_v, k_new, v_new):
    n_chunks = 4
    any_spec = pl.BlockSpec(memory_space=pl.ANY)
    return pl.pallas_call(
        functools.partial(_swa_shift_kernel, n_chunks=n_chunks),
        in_specs=[any_spec] * 4,
        out_specs=[any_spec] * 2,
        out_shape=[jax.ShapeDtypeStruct(cache_k.shape, cache_k.dtype), jax.ShapeDtypeStruct(cache_v.shape, cache_v.dtype)],
        scratch_shapes=[pltpu.SemaphoreType.DMA((2 * (n_chunks + 1),))],
        name="swa_shift",
    )(cache_k, cache_v, k_new, v_new)


ROUTE_GROUP_COL = 0
ROUTE_EXPERT_COL = 8


def _layer_norm(s, g, b):
    mu = jnp.mean(s, -1, keepdims=True)
    d = s - mu
    var = jnp.mean(d * d, -1, keepdims=True)
    return d * lax.rsqrt(var + LN_EPS) * g + b


def _route(x, w_hi, w_lo, bias):
    xh = x.astype(BF16)
    xl = (x - xh.astype(F32)).astype(BF16)
    logits = _dot(xh, w_hi) + (_dot(xl, w_hi) + _dot(xh, w_lo)) + bias
    col = lax.broadcasted_iota(jnp.int32, logits.shape, 1).astype(F32)
    big = float(LANES)
    is_g = col < N_EXPERT_GROUPS
    gl = jnp.where(is_g, logits, NEG_INF)
    gmax = jnp.max(gl, -1, keepdims=True)
    g_sel = jnp.min(jnp.where(gl == gmax, col, big), -1, keepdims=True)
    g_w = 1.0 / jnp.sum(jnp.where(is_g, jnp.exp(gl - gmax), 0.0), -1, keepdims=True)
    lo = ROUTE_EXPERT_COL + EXPERTS_PER_GROUP * g_sel
    el = jnp.where(jnp.logical_and(col >= lo, col < lo + EXPERTS_PER_GROUP), logits, NEG_INF)
    m1 = jnp.max(el, -1, keepdims=True)
    i1 = jnp.min(jnp.where(el == m1, col, big), -1, keepdims=True)
    el2 = jnp.where(col == i1, NEG_INF, el)
    m2 = jnp.max(el2, -1, keepdims=True)
    i2 = jnp.min(jnp.where(el2 == m2, col, big), -1, keepdims=True)
    e2 = jnp.exp(m2 - m1)
    w1 = g_w / (1.0 + e2)
    w2 = g_w * e2 / (1.0 + e2)
    out = jnp.where(col == 0, i1 - ROUTE_EXPERT_COL, 0.0)
    out = jnp.where(col == 1, i2 - ROUTE_EXPERT_COL, out)
    out = jnp.where(col == 2, w1, out)
    return jnp.where(col == 3, w2, out)


def _attn_out_tail(delta, h_ref, g_ref, b_ref, whi_ref, wlo_ref, rb_ref, h1_ref, route_ref):
    h1 = _layer_norm(DEEPNORM_ALPHA * h_ref[...] + delta, g_ref[...], b_ref[...])
    h1_ref[...] = h1
    route_ref[...] = _route(h1, whi_ref[...], wlo_ref[...], rb_ref[...])


def _attn_out_mla_kernel(mix_ref, mem_ref, w_ref, *tail):
    n_mix = mix_ref.shape[1]
    delta = _dot(mix_ref[...], w_ref[0:n_mix, :]) + _dot(mem_ref[...], w_ref[n_mix:, :])
    _attn_out_tail(delta, *tail)


def _attn_out_dil_kernel(o0_ref, o1_ref, o2_ref, l0_ref, l1_ref, l2_ref, mem_ref, w_ref, *tail):
    l0, l1, l2 = l0_ref[...], l1_ref[...], l2_ref[...]
    mx = jnp.maximum(jnp.maximum(l0, l1), l2)
    e0, e1, e2 = jnp.exp(l0 - mx), jnp.exp(l1 - mx), jnp.exp(l2 - mx)
    mix = (o0_ref[...] * e0 + o1_ref[...] * e1 + o2_ref[...] * e2) / (e0 + e1 + e2)
    n_mix = mix.shape[1]
    delta = _dot(mix.astype(BF16), w_ref[0:n_mix, :]) + _dot(mem_ref[...], w_ref[n_mix:, :])
    _attn_out_tail(delta, *tail)


def _attn_out(kernel_fn, parts, w, h, ln_g, ln_b, wr_hi, wr_lo, r_bias, name):
    m = h.shape[0]
    tm = min(_row_tile(m), 256)
    row = lambda i: (i, 0)
    fixed = lambda i: (0, 0)
    full = lambda a: pl.BlockSpec(a.shape, fixed)
    return pl.pallas_call(
        kernel_fn,
        grid=(m // tm,),
        in_specs=[pl.BlockSpec((tm, p.shape[1]), row) for p in parts]
        + [full(w), pl.BlockSpec((tm, D_MODEL), row), full(ln_g), full(ln_b), full(wr_hi), full(wr_lo), full(r_bias)],
        out_specs=[pl.BlockSpec((tm, D_MODEL), row), pl.BlockSpec((tm, LANES), row)],
        out_shape=[jax.ShapeDtypeStruct((m, D_MODEL), F32), jax.ShapeDtypeStruct((m, LANES), F32)],
        compiler_params=_cparams(1),
        name=name,
    )(*parts, w, h, ln_g, ln_b, wr_hi, wr_lo, r_bias)


def _moe_num_tiles(n_tokens):
    worst_rows = 2 * n_tokens + N_EXPERTS * (MOE_TM - 1)
    return -(-worst_rows // MOE_TM)


def _moe_dispatch(route):
    n = route.shape[0]
    ids = route[:, 0:2].astype(jnp.int32).reshape(-1)
    gates = route[:, 2:4].reshape(-1)
    onehot = (ids[:, None] == jnp.arange(N_EXPERTS, dtype=jnp.int32)[None, :]).astype(jnp.int32)
    counts = jnp.sum(onehot, 0)
    rank = jnp.sum((jnp.cumsum(onehot, 0) - onehot) * onehot, 1)
    padded = (counts + MOE_TM - 1) // MOE_TM * MOE_TM
    ends = jnp.cumsum(padded)
    pos = (ends - padded)[ids] + rank
    n_tiles = _moe_num_tiles(n)
    rows = n_tiles * MOE_TM
    row_token = jnp.zeros((rows,), jnp.int32).at[pos].set(jnp.arange(2 * n, dtype=jnp.int32) // 2)
    row_gate = jnp.zeros((rows,), F32).at[pos].set(gates)
    tile_start = jnp.arange(n_tiles, dtype=jnp.int32) * MOE_TM
    tile_expert = jnp.minimum(jnp.searchsorted(ends, tile_start, side="right"), N_EXPERTS - 1).astype(jnp.int32)
    n_valid = (ends[-1] // MOE_TM).astype(jnp.int32).reshape(1)
    return tile_expert, n_valid, row_token, row_gate.reshape(rows, 1), pos.astype(jnp.int32)


def _row_copy(src_hbm, src_row, dst_buf, dst_row, sem):
    return pltpu.make_async_copy(src_hbm.at[pl.ds(src_row, 1)], dst_buf.at[pl.ds(dst_row, 1)], sem)


def _moe_ffn_kernel(te_ref, nv_ref, rt_ref, x_hbm, gate_ref, wg_ref, wu_ref, wd_ref, y_ref,
                    xbuf, sem, wg_b, wu_b, wd_b):
    i = pl.program_id(0)
    n_valid = nv_ref[0]
    tm = xbuf.shape[1]

    def gather(tile, slot):
        def body(k, carry):
            _row_copy(x_hbm, rt_ref[tile * tm + k], xbuf.at[slot], k, sem.at[slot]).start()
            return carry
        lax.fori_loop(0, tm, body, 0)

    def gather_wait(slot):
        def body(k, carry):
            _row_copy(x_hbm, 0, xbuf.at[slot], k, sem.at[slot]).wait()
            return carry
        lax.fori_loop(0, tm, body, 0)

    @pl.when(jnp.logical_and(i == 0, n_valid > 0))
    def _():
        gather(0, 0)

    @pl.when(i + 1 < n_valid)
    def _():
        gather(i + 1, (i + 1) % 2)

    @pl.when(i < n_valid)
    def _():
        new_expert = jnp.logical_or(i == 0, te_ref[i] != te_ref[jnp.maximum(i - 1, 0)])

        @pl.when(new_expert)
        def _():
            wg_b[...] = wg_ref[0].astype(BF16)
            wu_b[...] = wu_ref[0].astype(BF16)
            wd_b[...] = wd_ref[0].astype(BF16)

        slot = i % 2
        gather_wait(slot)
        xb = xbuf[slot].astype(BF16)
        g = _dot(xb, wg_b[...])
        u = _dot(xb, wu_b[...])
        hid = g / (1.0 + jnp.exp(-g)) * u * gate_ref[...]
        y_ref[...] = _dot(hid.astype(BF16), wd_b[...])

    @pl.when(i >= n_valid)
    def _():
        y_ref[...] = jnp.zeros_like(y_ref)


def _moe_ffn(x, w_g, w_u, w_d, tile_expert, n_valid, row_token, row_gate):
    d = x.shape[1]
    hdim = w_g.shape[2]
    rows = row_token.shape[0]
    tm = MOE_TM
    n_tiles = rows // tm
    live = lambda i, te, nv, rt: jnp.minimum(i, jnp.maximum(nv[0] - 1, 0))
    grid_spec = pltpu.PrefetchScalarGridSpec(
        num_scalar_prefetch=3,
        grid=(n_tiles,),
        in_specs=[pl.BlockSpec(memory_space=pl.ANY),
                  pl.BlockSpec((tm, 1), lambda i, te, nv, rt: (live(i, te, nv, rt), 0)),
                  pl.BlockSpec((1, d, hdim), lambda i, te, nv, rt: (te[live(i, te, nv, rt)], 0, 0)),
                  pl.BlockSpec((1, d, hdim), lambda i, te, nv, rt: (te[live(i, te, nv, rt)], 0, 0)),
                  pl.BlockSpec((1, hdim, d), lambda i, te, nv, rt: (te[live(i, te, nv, rt)], 0, 0))],
        out_specs=pl.BlockSpec((tm, d), lambda i, te, nv, rt: (i, 0)),
        scratch_shapes=[pltpu.VMEM((2, tm, d), F32), pltpu.SemaphoreType.DMA((2,)),
                        pltpu.VMEM((d, hdim), BF16), pltpu.VMEM((d, hdim), BF16), pltpu.VMEM((hdim, d), BF16)],
    )
    return pl.pallas_call(
        _moe_ffn_kernel,
        grid_spec=grid_spec,
        out_shape=jax.ShapeDtypeStruct((rows, d), F32),
        compiler_params=_cparams(1),
        name="moe_ffn",
    )(tile_expert, n_valid, row_token, x, row_gate, w_g, w_u, w_d)


def _moe_combine_kernel(pos_ref, h_ref, y_hbm, g_ref, b_ref, o_ref, ybuf, sem, *, tok0):
    i = pl.program_id(0)
    n_steps = pl.num_programs(0)
    tm = h_ref.shape[0]

    def gather(step, slot):
        def body(k, carry):
            base = 2 * (tok0 + step * tm + k)
            _row_copy(y_hbm, pos_ref[base], ybuf.at[slot, 0], k, sem.at[slot]).start()
            _row_copy(y_hbm, pos_ref[base + 1], ybuf.at[slot, 1], k, sem.at[slot]).start()
            return carry
        lax.fori_loop(0, tm, body, 0)

    def gather_wait(slot):
        def body(k, carry):
            _row_copy(y_hbm, 0, ybuf.at[slot, 0], k, sem.at[slot]).wait()
            _row_copy(y_hbm, 0, ybuf.at[slot, 1], k, sem.at[slot]).wait()
            return carry
        lax.fori_loop(0, tm, body, 0)

    @pl.when(i == 0)
    def _():
        gather(0, 0)

    @pl.when(i + 1 < n_steps)
    def _():
        gather(i + 1, (i + 1) % 2)

    slot = i % 2
    gather_wait(slot)
    s = DEEPNORM_ALPHA * h_ref[...] + (ybuf[slot, 0] + ybuf[slot, 1])
    o_ref[...] = _layer_norm(s, g_ref[...], b_ref[...])


def _moe_combine(pos, h1, y_sorted, ln_g, ln_b, tok0, n_rows):
    d = h1.shape[1]
    tm = min(_row_tile(n_rows), 256)
    assert tok0 % tm == 0
    blk0 = tok0 // tm
    grid_spec = pltpu.PrefetchScalarGridSpec(
        num_scalar_prefetch=1,
        grid=(n_rows // tm,),
        in_specs=[pl.BlockSpec((tm, d), lambda i, p: (blk0 + i, 0)),
                  pl.BlockSpec(memory_space=pl.ANY),
                  pl.BlockSpec((1, d), lambda i, p: (0, 0)), pl.BlockSpec((1, d), lambda i, p: (0, 0))],
        out_specs=pl.BlockSpec((tm, d), lambda i, p: (i, 0)),
        scratch_shapes=[pltpu.VMEM((2, 2, tm, d), F32), pltpu.SemaphoreType.DMA((2,))],
    )
    return pl.pallas_call(
        functools.partial(_moe_combine_kernel, tok0=tok0),
        grid_spec=grid_spec,
        out_shape=jax.ShapeDtypeStruct((n_rows, d), F32),
        compiler_params=_cparams(1),
        name="moe_combine",
    )(pos, h1, y_sorted, ln_g, ln_b)


def _router_weights(w_rg, b_rg, w_re, b_re):
    d = w_rg.shape[0]
    w = jnp.zeros((d, LANES), F32)
    w = w.at[:, ROUTE_GROUP_COL:ROUTE_GROUP_COL + N_EXPERT_GROUPS].set(w_rg)
    w = w.at[:, ROUTE_EXPERT_COL:ROUTE_EXPERT_COL + N_EXPERTS].set(w_re)
    bias = jnp.zeros((1, LANES), F32)
    bias = bias.at[0, ROUTE_GROUP_COL:ROUTE_GROUP_COL + N_EXPERT_GROUPS].set(b_rg)
    bias = bias.at[0, ROUTE_EXPERT_COL:ROUTE_EXPERT_COL + N_EXPERTS].set(b_re)
    w_hi = w.astype(BF16)
    w_lo = (w - w_hi.astype(F32)).astype(BF16)
    return w_hi, w_lo, bias


def _moe_block(h1_p, route_p, h1_s, route_s, w_g, w_u, w_d, ln_g, ln_b):
    n_p, n_s = h1_p.shape[0], h1_s.shape[0]
    h1 = jnp.concatenate([h1_p, h1_s], axis=0)
    route = jnp.concatenate([route_p, route_s], axis=0)
    tile_expert, n_valid, row_token, row_gate, pos = _moe_dispatch(route)
    y_sorted = _moe_ffn(h1, w_g, w_u, w_d, tile_expert, n_valid, row_token, row_gate)
    out_p = _moe_combine(pos, h1, y_sorted, ln_g, ln_b, 0, n_p)
    out_s = _moe_combine(pos, h1, y_sorted, ln_g, ln_b, n_p, n_s)
    return out_p, out_s


def kernel(x_prompt, x_sample, cache_mla_latent, cache_mla_krope, cache_swa_k, cache_swa_v, cache_mem_k, cache_mem_v, page_table, mem_prompt, w_in_a, q_norm_a, kv_norm_a, w_q_up_a, w_kv_up_a, w_out_a, w_in_b, w_kv_shared, w_out_b, w_mem_kv, ln1_g, ln1_b, ln2_g, ln2_b, w_router_group, b_router_group, w_router_expert, b_router_expert, w_exp_gate, w_exp_up, w_exp_down):
    bp, tp, d = x_prompt.shape
    bs, ts, _ = x_sample.shape
    assert ts == 1 and w_in_a.shape[0] == 1 and w_in_b.shape[0] == 1
    n_p, n_s = bp * tp, bs * ts
    n_mem = mem_prompt.shape[1]
    wbuf = cache_swa_k.shape[1]
    past_len = page_table.shape[1] * PAGE_SIZE
    assert wbuf == DIL_PATTERNS[-1][0]

    pos_p = jnp.arange(tp, dtype=jnp.int32)
    pos_s = jnp.full((n_s,), past_len, jnp.int32)
    tm_p = _row_tile(n_p)
    assert tp % tm_p == 0
    tab_blocks_p = tp // tm_p
    tabs_r_p = _rope_tables(pos_p, ROPE_DIM, LANES)
    tabs_r_s = _rope_tables(pos_s, ROPE_DIM, LANES)
    tabs_h_p = _rope_tables(pos_p, HEAD_DIM, LANES)
    tabs_h_s = _rope_tables(pos_s, HEAD_DIM, LANES)

    hp = x_prompt.reshape(n_p, d)
    hs = x_sample.reshape(n_s, d)
    ln = lambda a, l: a[l].reshape(1, d)

    mem2d = mem_prompt.reshape(bp * n_mem, d)
    mkv = [_matmul(mem2d, w_mem_kv[l].astype(BF16)).reshape(bp, n_mem, 2 * MEM_DIM) for l in range(DEPTH)]

    w_in = w_in_a[0]
    o_kr = Q_LORA + KV_LORA
    w_in = jnp.concatenate([w_in[:, :o_kr + ROPE_DIM], jnp.zeros((d, LANES - ROPE_DIM), F32),
                            w_in[:, o_kr + ROPE_DIM:]], axis=1).astype(BF16)
    w_q = jnp.concatenate([w_q_up_a[0], jnp.zeros((Q_LORA, MIX_HEADS, MLA_QK - NOPE_DIM - ROPE_DIM), F32)], axis=-1)
    w_q = w_q.reshape(Q_LORA, MIX_HEADS * MLA_QK).astype(BF16)
    w_kv = w_kv_up_a[0].reshape(KV_LORA, MIX_HEADS * (NOPE_DIM + V_DIM)).astype(BF16)
    w_uk_t = jnp.transpose(w_kv_up_a[0][..., :NOPE_DIM], (1, 2, 0)).astype(BF16)
    w_uv = jnp.transpose(w_kv_up_a[0][..., NOPE_DIM:], (1, 0, 2)).astype(BF16)
    q_g, kv_g = q_norm_a[0].reshape(1, Q_LORA), kv_norm_a[0].reshape(1, KV_LORA)

    cq_p, lat_p, kr_p, qm_p = _mla_in_proj(hp, w_in, q_g, kv_g, tabs_r_p, tab_blocks_p)
    cq_s, lat_s, kr_s, qm_s = _mla_in_proj(hs, w_in, q_g, kv_g, tabs_r_s, 1)
    q_p = _q_up(cq_p, w_q, tabs_r_p, tab_blocks_p)
    q_s = _q_up(cq_s, w_q, tabs_r_s, 1)

    k_full, v_full = _kv_up(lat_p, kr_p, w_kv)
    mix_p = _mla_flash(q_p.reshape(bp, tp, -1), k_full.reshape(bp, tp, -1), v_full.reshape(bp, tp, -1)).reshape(n_p, -1)

    hpad = 16
    q_s3 = q_s.reshape(n_s, MIX_HEADS, MLA_QK)
    qlat = _head_matmul(jnp.transpose(q_s3[:, :, :NOPE_DIM], (1, 0, 2)), w_uk_t, BF16)
    qlat = jnp.pad(jnp.transpose(qlat, (1, 0, 2)), ((0, 0), (0, hpad - MIX_HEADS), (0, 0)))
    qr = jnp.pad(q_s3[:, :, NOPE_DIM:], ((0, 0), (0, hpad - MIX_HEADS), (0, 0)))
    o_lat = _mla_decode(qlat, qr, lat_s.reshape(n_s, 1, KV_LORA), kr_s.reshape(n_s, 1, LANES),
                        cache_mla_latent, cache_mla_krope, page_table, 0)
    mix_s = _head_matmul(jnp.transpose(o_lat[:, :MIX_HEADS], (1, 0, 2)), w_uv, BF16)
    mix_s = jnp.transpose(mix_s, (1, 0, 2)).reshape(n_s, MIX_HEADS * V_DIM)

    mem_p = _mem_attn_prompt(qm_p.reshape(bp, tp, MEM_DIM), mkv[0]).reshape(n_p, MEM_DIM)
    mem_s = _mem_attn_sample(qm_s, cache_mem_k, cache_mem_v, 0)

    router = [_router_weights(w_router_group[l], b_router_group[l], w_router_expert[l], b_router_expert[l])
              for l in range(DEPTH)]
    w_out = w_out_a[0].astype(BF16)
    h1_p, route_p = _attn_out(_attn_out_mla_kernel, [mix_p, mem_p], w_out, hp, ln(ln1_g, 0), ln(ln1_b, 0),
                              *router[0], name="attn_out_mla")
    h1_s, route_s = _attn_out(_attn_out_mla_kernel, [mix_s, mem_s], w_out, hs, ln(ln1_g, 0), ln(ln1_b, 0),
                              *router[0], name="attn_out_mla")
    hp, hs = _moe_block(h1_p, route_p, h1_s, route_s, w_exp_gate[0], w_exp_up[0], w_exp_down[0],
                        ln(ln2_g, 0), ln(ln2_b, 0))

    n_q = MIX_HEADS * HEAD_DIM
    w_b = w_in_b[0].astype(BF16)
    w_kvs = w_kv_shared.astype(BF16)
    q_segs = tuple((g * DIL_KV_DIM, DIL_KV_DIM, True, HEAD_SCALE, 0, g * DIL_KV_DIM) for g in range(N_DIL_GROUPS))
    q_segs += ((n_q, MEM_DIM, False, HEAD_SCALE, 1, 0),)
    kv_segs = ((0, DIL_KV_DIM, True, 1.0, 0, 0), (DIL_KV_DIM, DIL_KV_DIM, False, 1.0, 1, 0))
    qd_p, qm_p = _proj_rope(hp, w_b, tabs_h_p, tab_blocks_p, q_segs, [(n_q, BF16), (MEM_DIM, BF16)])
    qd_s, qm_s = _proj_rope(hs, w_b, tabs_h_s, 1, q_segs, [(n_q, F32), (MEM_DIM, BF16)])
    k_p, v_p = _proj_rope(hp, w_kvs, tabs_h_p, tab_blocks_p, kv_segs, [(DIL_KV_DIM, F32), (DIL_KV_DIM, F32)])
    k_s, v_s = _proj_rope(hs, w_kvs, tabs_h_s, 1, kv_segs, [(DIL_KV_DIM, F32), (DIL_KV_DIM, F32)])

    k_pb, v_pb = k_p.astype(BF16), v_p.astype(BF16)
    outs, lses = [], []
    for g, (_, dil) in enumerate(DIL_PATTERNS):
        def split(a):
            a = a.reshape(bp, tp // dil, dil, DIL_KV_DIM)
            return jnp.transpose(a, (0, 2, 1, 3)).reshape(bp * dil, tp // dil, DIL_KV_DIM)

        def merge(a):
            a = a.reshape(bp, dil, tp // dil, DIL_KV_DIM)
            return jnp.transpose(a, (0, 2, 1, 3)).reshape(n_p, DIL_KV_DIM)

        o_g, lse_g = _band_attn(split(qd_p[:, g * DIL_KV_DIM:(g + 1) * DIL_KV_DIM]), split(k_pb), split(v_pb))
        outs.append(merge(o_g))
        lses.append(merge(lse_g))

    mix_s = _dil_sample(qd_s, k_s, v_s, cache_swa_k, cache_swa_v)
    mem_p = _mem_attn_prompt(qm_p.reshape(bp, tp, MEM_DIM), mkv[1]).reshape(n_p, MEM_DIM)
    mem_s = _mem_attn_sample(qm_s, cache_mem_k, cache_mem_v, 1)

    w_out = w_out_b[0].astype(BF16)
    h1_p, route_p = _attn_out(_attn_out_dil_kernel, outs + lses + [mem_p], w_out, hp, ln(ln1_g, 1), ln(ln1_b, 1),
                              *router[1], name="attn_out_dil")
    h1_s, route_s = _attn_out(_attn_out_mla_kernel, [mix_s, mem_s], w_out, hs, ln(ln1_g, 1), ln(ln1_b, 1),
                              *router[1], name="attn_out_mla")
    hp, hs = _moe_block(h1_p, route_p, h1_s, route_s, w_exp_gate[1], w_exp_up[1], w_exp_down[1],
                        ln(ln2_g, 1), ln(ln2_b, 1))

    kv4 = lambda a, n: a.reshape(n, -1, DIL_KV_HEADS, HEAD_DIM)
    new_k_s, new_v_s = _swa_shift(cache_swa_k, cache_swa_v, kv4(k_s, bs), kv4(v_s, bs))
    keep_p = min(wbuf, tp)
    k_p4, v_p4 = kv4(k_p, bp), kv4(v_p, bp)
    mem_k = jnp.stack([m[:, :, :MEM_DIM].reshape(bp, n_mem, MEM_HEADS, HEAD_DIM) for m in mkv], axis=2)
    mem_v = jnp.stack([m[:, :, MEM_DIM:].reshape(bp, n_mem, MEM_HEADS, HEAD_DIM) for m in mkv], axis=2)
    return (hp.reshape(bp, tp, d), hs.reshape(bs, ts, d),
            lat_p.reshape(bp, tp, 1, KV_LORA), kr_p[:, :ROPE_DIM].reshape(bp, tp, 1, ROPE_DIM),
            lat_s.reshape(bs, ts, 1, KV_LORA), kr_s[:, :ROPE_DIM].reshape(bs, ts, 1, ROPE_DIM),
            k_p4[:, tp - keep_p:], v_p4[:, tp - keep_p:], new_k_s, new_v_s, mem_k, mem_v)
```

```python
import functools

import jax
import jax.numpy as jnp
import numpy as np
from jax import lax
from jax.experimental import pallas as pl
from jax.experimental.pallas import tpu as pltpu

BF16 = jnp.bfloat16
F32 = jnp.float32

D_MODEL = 2048
DEPTH = 2
HEAD_DIM = 128
MIX_HEADS = 12
MEM_HEADS = 4
MEM_DIM = MEM_HEADS * HEAD_DIM
Q_LORA = 512
KV_LORA = 256
NOPE_DIM = 128
ROPE_DIM = 64
V_DIM = 128
PAGE_SIZE = 128
DIL_PATTERNS = ((128, 1), (512, 4), (2048, 16))
N_DIL_GROUPS = len(DIL_PATTERNS)
DIL_KV_HEADS = MIX_HEADS // N_DIL_GROUPS
DIL_KV_DIM = DIL_KV_HEADS * HEAD_DIM
DIL_KEYS = 128
N_EXPERT_GROUPS = 4
EXPERTS_PER_GROUP = 8
N_EXPERTS = N_EXPERT_GROUPS * EXPERTS_PER_GROUP
EXPERT_HIDDEN = 512
ROPE_THETA = 10000.0
LN_EPS = 1e-5
RMS_EPS = 1e-6
DEEPNORM_ALPHA = (2.0 * DEPTH) ** 0.25
MLA_SCALE = (NOPE_DIM + ROPE_DIM) ** -0.5
HEAD_SCALE = HEAD_DIM ** -0.5
MLA_EXP2_SCALE = MLA_SCALE * float(np.log2(np.e))

LANES = 128
MLA_QK = 2 * LANES
VMEM_LIMIT = 56 * 1024 * 1024
MOE_TM = 256
MLA_PAGES_PER_STEP = 16
NEG_INF = float("-inf")


def _cparams(n_grid):
    return pltpu.CompilerParams(dimension_semantics=("arbitrary",) * n_grid, vmem_limit_bytes=VMEM_LIMIT)


def _row_tile(m):
    for t in (512, 256, 128, 64, 32, 16, 8):
        if m % t == 0:
            return t
    raise ValueError(f"row count {m} not a multiple of 8")


def _dot(a, b):
    return jnp.dot(a, b, preferred_element_type=F32)


def _dot_t(a, b):
    return lax.dot_general(a, b, (((1,), (1,)), ((), ())), preferred_element_type=F32)


def _rope_tables(pos, dim, width):
    half = dim // 2
    inv_freq = ROPE_THETA ** (-jnp.arange(0, dim, 2, dtype=F32) / dim)
    ang = pos.astype(F32)[:, None] * inv_freq[None, :]
    cos, sin = jnp.cos(ang), jnp.sin(ang)
    z = jnp.zeros((pos.shape[0], width - dim), F32)
    zh = jnp.zeros_like(cos)
    c = jnp.concatenate([cos, cos, z], axis=-1)
    s_lo = jnp.concatenate([-sin, zh, z], axis=-1)
    s_hi = jnp.concatenate([zh, sin, z], axis=-1)
    return c, s_lo, s_hi


def _rope_apply(x, c, s_lo, s_hi, half):
    width = x.shape[-1]
    return x * c + pltpu.roll(x, width - half, 1) * s_lo + pltpu.roll(x, half, 1) * s_hi


def _mm_kernel(x_ref, w_ref, o_ref):
    o_ref[...] = _dot(x_ref[...].astype(BF16), w_ref[...]).astype(o_ref.dtype)


def _matmul(x, w, out_dtype=F32):
    m, k = x.shape
    n = w.shape[1]
    tm = _row_tile(m)
    return pl.pallas_call(
        _mm_kernel,
        grid=(m // tm,),
        in_specs=[pl.BlockSpec((tm, k), lambda i: (i, 0)), pl.BlockSpec((k, n), lambda i: (0, 0))],
        out_specs=pl.BlockSpec((tm, n), lambda i: (i, 0)),
        out_shape=jax.ShapeDtypeStruct((m, n), out_dtype),
        compiler_params=_cparams(1),
        name="matmul",
    )(x, w)


def _bmm_kernel(x_ref, w_ref, o_ref):
    o_ref[0] = _dot(x_ref[0].astype(BF16), w_ref[0]).astype(o_ref.dtype)


def _head_matmul(x, w, out_dtype):
    h, m, k = x.shape
    n = w.shape[2]
    return pl.pallas_call(
        _bmm_kernel,
        grid=(h,),
        in_specs=[pl.BlockSpec((1, m, k), lambda i: (i, 0, 0)), pl.BlockSpec((1, k, n), lambda i: (i, 0, 0))],
        out_specs=pl.BlockSpec((1, m, n), lambda i: (i, 0, 0)),
        out_shape=jax.ShapeDtypeStruct((h, m, n), out_dtype),
        compiler_params=_cparams(1),
        name="head_matmul",
    )(x, w)


def _mla_in_kernel(x_ref, w_ref, qg_ref, kvg_ref, c_ref, slo_ref, shi_ref, cq_ref, lat_ref, kr_ref, qm_ref):
    xb = x_ref[...].astype(BF16)
    c_q = _dot(xb, w_ref[:, 0:Q_LORA])
    cq_ref[...] = (c_q * lax.rsqrt(jnp.mean(c_q * c_q, -1, keepdims=True) + RMS_EPS) * qg_ref[...]).astype(cq_ref.dtype)
    o = Q_LORA
    c_kv = _dot(xb, w_ref[:, o:o + KV_LORA])
    lat_ref[...] = c_kv * lax.rsqrt(jnp.mean(c_kv * c_kv, -1, keepdims=True) + RMS_EPS) * kvg_ref[...]
    o += KV_LORA
    k_rope = _dot(xb, w_ref[:, o:o + LANES])
    kr_ref[...] = _rope_apply(k_rope, c_ref[...], slo_ref[...], shi_ref[...], ROPE_DIM // 2)
    o += LANES
    qm_ref[...] = _dot(xb, w_ref[:, o:o + MEM_DIM]).astype(qm_ref.dtype)


def _mla_in_proj(x, w, q_g, kv_g, tabs, n_tab_blocks):
    m, d = x.shape
    tm = _row_tile(m)
    n = w.shape[1]
    row = lambda i: (i, 0)
    fixed = lambda i: (0, 0)
    tab = lambda i: (i % n_tab_blocks, 0)
    return pl.pallas_call(
        _mla_in_kernel,
        grid=(m // tm,),
        in_specs=[pl.BlockSpec((tm, d), row), pl.BlockSpec((d, n), fixed),
                  pl.BlockSpec((1, Q_LORA), fixed), pl.BlockSpec((1, KV_LORA), fixed),
                  pl.BlockSpec((tm, LANES), tab), pl.BlockSpec((tm, LANES), tab), pl.BlockSpec((tm, LANES), tab)],
        out_specs=[pl.BlockSpec((tm, Q_LORA), row), pl.BlockSpec((tm, KV_LORA), row),
                   pl.BlockSpec((tm, LANES), row), pl.BlockSpec((tm, MEM_DIM), row)],
        out_shape=[jax.ShapeDtypeStruct((m, Q_LORA), BF16), jax.ShapeDtypeStruct((m, KV_LORA), F32),
                   jax.ShapeDtypeStruct((m, LANES), F32), jax.ShapeDtypeStruct((m, MEM_DIM), BF16)],
        compiler_params=_cparams(1),
        name="mla_in_proj",
    )(x, w, q_g, kv_g, *tabs)


def _q_up_kernel(cq_ref, w_ref, c_ref, slo_ref, shi_ref, q_ref):
    cq = cq_ref[...]
    c, slo, shi = c_ref[...], slo_ref[...], shi_ref[...]
    for h in range(MIX_HEADS):
        r = _dot(cq, w_ref[:, h * MLA_QK:(h + 1) * MLA_QK])
        q_ref[:, h * MLA_QK:h * MLA_QK + LANES] = r[:, :LANES].astype(q_ref.dtype)
        roped = _rope_apply(r[:, LANES:], c, slo, shi, ROPE_DIM // 2)
        q_ref[:, h * MLA_QK + LANES:(h + 1) * MLA_QK] = roped.astype(q_ref.dtype)


def _q_up(cq, w, tabs, n_tab_blocks):
    m, k = cq.shape
    tm = _row_tile(m)
    n = w.shape[1]
    row = lambda i: (i, 0)
    fixed = lambda i: (0, 0)
    tab = lambda i: (i % n_tab_blocks, 0)
    return pl.pallas_call(
        _q_up_kernel,
        grid=(m // tm,),
        in_specs=[pl.BlockSpec((tm, k), row), pl.BlockSpec((k, n), fixed),
                  pl.BlockSpec((tm, LANES), tab), pl.BlockSpec((tm, LANES), tab), pl.BlockSpec((tm, LANES), tab)],
        out_specs=pl.BlockSpec((tm, n), row),
        out_shape=jax.ShapeDtypeStruct((m, n), BF16),
        compiler_params=_cparams(1),
        name="mla_q_up",
    )(cq, w, *tabs)


def _kv_up_kernel(lat_ref, kr_ref, w_ref, k_ref, v_ref):
    lat = lat_ref[...].astype(BF16)
    kr = kr_ref[...].astype(BF16)
    for h in range(MIX_HEADS):
        r = _dot(lat, w_ref[:, h * 2 * LANES:(h + 1) * 2 * LANES])
        k_ref[:, h * MLA_QK:h * MLA_QK + LANES] = r[:, :LANES].astype(BF16)
        k_ref[:, h * MLA_QK + LANES:(h + 1) * MLA_QK] = kr
        v_ref[:, h * V_DIM:(h + 1) * V_DIM] = r[:, LANES:].astype(BF16)


def _kv_up(lat, kr, w):
    m = lat.shape[0]
    tm = _row_tile(m)
    row = lambda i: (i, 0)
    return pl.pallas_call(
        _kv_up_kernel,
        grid=(m // tm,),
        in_specs=[pl.BlockSpec((tm, KV_LORA), row), pl.BlockSpec((tm, LANES), row),
                  pl.BlockSpec(w.shape, lambda i: (0, 0))],
        out_specs=[pl.BlockSpec((tm, MIX_HEADS * MLA_QK), row), pl.BlockSpec((tm, MIX_HEADS * V_DIM), row)],
        out_shape=[jax.ShapeDtypeStruct((m, MIX_HEADS * MLA_QK), BF16),
                   jax.ShapeDtypeStruct((m, MIX_HEADS * V_DIM), BF16)],
        compiler_params=_cparams(1),
        name="mla_kv_up",
    )(lat, kr, w)


def _mla_flash_kernel(q_ref, k_ref, v_ref, o_ref, *, tq):
    i = pl.program_id(2)
    q = q_ref[0]

    def step(j, carry, masked):
        m, l, acc = carry
        start = pl.multiple_of(j * tq, tq)
        s = _dot_t(q, k_ref[0, pl.ds(start, tq), :])
        if masked:
            rows = lax.broadcasted_iota(jnp.int32, s.shape, 0)
            cols = lax.broadcasted_iota(jnp.int32, s.shape, 1)
            s = jnp.where(cols <= rows, s, NEG_INF)
        m_new = jnp.maximum(m, jnp.max(s, -1, keepdims=True))
        corr = jnp.exp2((m - m_new) * MLA_EXP2_SCALE)
        p = jnp.exp2((s - m_new) * MLA_EXP2_SCALE)
        l = l * corr + jnp.sum(p, -1, keepdims=True)
        acc = acc * corr + _dot(p.astype(BF16), v_ref[0, pl.ds(start, tq), :])
        return m_new, l, acc

    init = (jnp.full((tq, 1), NEG_INF, F32), jnp.zeros((tq, 1), F32), jnp.zeros((tq, V_DIM), F32))
    carry = lax.fori_loop(0, i, lambda j, c: step(j, c, False), init)
    _, l, acc = step(i, carry, True)
    o_ref[0] = (acc / l).astype(o_ref.dtype)


def _mla_flash(q, k, v):
    b, t, _ = q.shape
    tq = _row_tile(t)
    return pl.pallas_call(
        functools.partial(_mla_flash_kernel, tq=tq),
        grid=(b, MIX_HEADS, t // tq),
        in_specs=[pl.BlockSpec((1, tq, MLA_QK), lambda b_, h, i: (b_, i, h)),
                  pl.BlockSpec((1, t, MLA_QK), lambda b_, h, i: (b_, 0, h)),
                  pl.BlockSpec((1, t, V_DIM), lambda b_, h, i: (b_, 0, h))],
        out_specs=pl.BlockSpec((1, tq, V_DIM), lambda b_, h, i: (b_, i, h)),
        out_shape=jax.ShapeDtypeStruct((b, t, MIX_HEADS * V_DIM), BF16),
        compiler_params=_cparams(3),
        name="mla_flash",
    )(q, k, v)


def _mem_attn_kernel(q_ref, k_ref, v_ref, o_ref):
    for h in range(MEM_HEADS):
        sl = slice(h * HEAD_DIM, (h + 1) * HEAD_DIM)
        s = _dot_t(q_ref[0, :, sl], k_ref[0, :, sl].astype(BF16)) * HEAD_SCALE
        p = jnp.exp(s - jnp.max(s, -1, keepdims=True))
        p = p / jnp.sum(p, -1, keepdims=True)
        o_ref[0, :, sl] = _dot(p.astype(BF16), v_ref[0, :, sl].astype(BF16)).astype(o_ref.dtype)


def _mem_attn_prompt(q, mkv):
    b, t, _ = q.shape
    n_mem = mkv.shape[1]
    tq = _row_tile(t)
    return pl.pallas_call(
        _mem_attn_kernel,
        grid=(b, t // tq),
        in_specs=[pl.BlockSpec((1, tq, MEM_DIM), lambda b_, i: (b_, i, 0)),
                  pl.BlockSpec((1, n_mem, MEM_DIM), lambda b_, i: (b_, 0, 0)),
                  pl.BlockSpec((1, n_mem, MEM_DIM), lambda b_, i: (b_, 0, 1))],
        out_specs=pl.BlockSpec((1, tq, MEM_DIM), lambda b_, i: (b_, i, 0)),
        out_shape=jax.ShapeDtypeStruct((b, t, MEM_DIM), BF16),
        compiler_params=_cparams(2),
        name="mem_attn_prompt",
    )(q, mkv, mkv)


def _head_rows(width, rows=8):
    r = lax.broadcasted_iota(jnp.int32, (rows, width), 0)
    c = lax.broadcasted_iota(jnp.int32, (rows, width), 1)
    return (c // HEAD_DIM) == r


def _single_query_attend(q_row, k_heads, v_heads, valid_rows, normalise_first):
    n = k_heads[0].shape[0]
    row_id = lax.broadcasted_iota(jnp.int32, (8, HEAD_DIM), 0)
    s = None
    for h, k_h in enumerate(k_heads):
        q_h = jnp.broadcast_to(q_row[:, h * HEAD_DIM:(h + 1) * HEAD_DIM], (8, HEAD_DIM))
        s_h = _dot_t(k_h, jnp.where(row_id == h, q_h, 0.0).astype(BF16))
        s = s_h if s is None else s + s_h
    s = s * HEAD_SCALE
    if valid_rows < n:
        s = jnp.where(lax.broadcasted_iota(jnp.int32, s.shape, 0) < valid_rows, s, NEG_INF)
    m = jnp.max(s, 0, keepdims=True)
    e = jnp.exp(s - m)
    l = jnp.sum(e, 0, keepdims=True)
    eb = (e / l if normalise_first else e).astype(BF16)
    o = jnp.concatenate(
        [lax.dot_general(eb, v_h, (((0,), (0,)), ((), ())), preferred_element_type=F32) for v_h in v_heads], axis=-1)
    eye = lax.broadcasted_iota(jnp.int32, (8, 8), 0) == lax.broadcasted_iota(jnp.int32, (8, 8), 1)
    lse = jnp.sum(jnp.where(eye, jnp.broadcast_to(m + jnp.log(l), (8, 8)), 0.0), -1, keepdims=True)
    if normalise_first:
        return o, lse
    l_col = jnp.sum(jnp.where(eye, jnp.broadcast_to(l, (8, 8)), 0.0), -1, keepdims=True)
    return o / l_col, lse


def _collapse_heads(o):
    return jnp.sum(jnp.where(_head_rows(o.shape[1]), o, 0.0), 0, keepdims=True)


def _mem_attn_sample_kernel(q_ref, k_ref, v_ref, o_ref, *, bb):
    n_mem = k_ref.shape[1]
    for j in range(bb):
        k = [k_ref[j, :, h, :].astype(BF16) for h in range(MEM_HEADS)]
        v = [v_ref[j, :, h, :].astype(BF16) for h in range(MEM_HEADS)]
        o, _ = _single_query_attend(q_ref[pl.ds(j, 1), :].astype(F32), k, v, n_mem, True)
        o_ref[pl.ds(j, 1), :] = _collapse_heads(o).astype(o_ref.dtype)


def _mem_attn_sample(q, cache_k, cache_v, layer):
    b = q.shape[0]
    n_mem = cache_k.shape[1]
    bb = 8
    cspec = pl.BlockSpec((bb, n_mem, None, MEM_HEADS, HEAD_DIM), lambda i: (i, 0, layer, 0, 0))
    return pl.pallas_call(
        functools.partial(_mem_attn_sample_kernel, bb=bb),
        grid=(b // bb,),
        in_specs=[pl.BlockSpec((bb, MEM_DIM), lambda i: (i, 0)), cspec, cspec],
        out_specs=pl.BlockSpec((bb, MEM_DIM), lambda i: (i, 0)),
        out_shape=jax.ShapeDtypeStruct((b, MEM_DIM), BF16),
        compiler_params=_cparams(1),
        name="mem_attn_sample",
    )(q, cache_k, cache_v)


def _mla_decode_kernel(pt_ref, qlat_ref, qr_ref, latn_ref, krn_ref, *refs, pages, layer, n_layers):
    del pt_ref
    lat_refs, kr_refs = refs[:pages], refs[pages:2 * pages]
    o_ref, m_sc, l_sc, acc_sc = refs[2 * pages:]
    c = pl.program_id(1)
    qlat = qlat_ref[0]
    qr_full = qr_ref[0]
    qr = qr_full[:, :ROPE_DIM]

    def page_latent(j):
        stride = 2 * n_layers
        lo = lat_refs[j][pl.ds(2 * layer, PAGE_SIZE, stride=stride), :]
        hi = lat_refs[j][pl.ds(2 * layer + 1, PAGE_SIZE, stride=stride), :]
        return jnp.concatenate([lo, hi], axis=-1).astype(BF16)

    @pl.when(c == 0)
    def _():
        latn = latn_ref[0]
        s0 = (jnp.sum(qlat.astype(F32) * latn.astype(BF16).astype(F32), -1, keepdims=True)
              + jnp.sum(qr_full.astype(F32) * krn_ref[0].astype(BF16).astype(F32), -1, keepdims=True)) * MLA_SCALE
        m_sc[...] = s0
        l_sc[...] = jnp.ones_like(s0)
        acc_sc[...] = jnp.broadcast_to(latn.astype(BF16).astype(F32), acc_sc.shape)

    lats = [page_latent(j) for j in range(pages)]
    scores = [(_dot_t(qlat, lats[j]) + _dot(qr, kr_refs[j][...].astype(BF16))) * MLA_SCALE for j in range(pages)]
    m_old = m_sc[...]
    tile_max = functools.reduce(jnp.maximum, scores)
    m_new = jnp.maximum(m_old, jnp.max(tile_max, -1, keepdims=True))
    corr = jnp.exp(m_old - m_new)
    l = l_sc[...] * corr
    acc = acc_sc[...] * corr
    for j in range(pages):
        p = jnp.exp(scores[j] - m_new)
        l = l + jnp.sum(p, -1, keepdims=True)
        acc = acc + _dot(p.astype(BF16), lats[j])
    m_sc[...] = m_new
    l_sc[...] = l
    acc_sc[...] = acc

    @pl.when(c == pl.num_programs(1) - 1)
    def _():
        o_ref[0] = acc / l


def _mla_decode(qlat, qr, lat_new, kr_new, cache_lat, cache_kr, page_table, layer):
    b, hp, _ = qlat.shape
    n_pages = page_table.shape[1]
    n_phys, _, n_layers, _ = cache_lat.shape
    pages = min(MLA_PAGES_PER_STEP, n_pages)
    assert n_pages % pages == 0
    per_b = lambda i, c, pt: (i, 0, 0)
    page_rows = PAGE_SIZE * n_layers * (KV_LORA // LANES)
    cache_lat = cache_lat.reshape(n_phys * page_rows, LANES)
    cache_kr = jnp.transpose(cache_kr, (0, 2, 3, 1))

    def lat_map(j):
        return lambda i, c, pt: (pt[i * n_pages + c * pages + j], 0)

    def kr_map(j):
        return lambda i, c, pt: (pt[i * n_pages + c * pages + j], layer, 0, 0)

    lat_specs = [pl.BlockSpec((page_rows, LANES), lat_map(j)) for j in range(pages)]
    kr_specs = [pl.BlockSpec((None, None, ROPE_DIM, PAGE_SIZE), kr_map(j)) for j in range(pages)]
    grid_spec = pltpu.PrefetchScalarGridSpec(
        num_scalar_prefetch=1,
        grid=(b, n_pages // pages),
        in_specs=[pl.BlockSpec((1, hp, KV_LORA), per_b), pl.BlockSpec((1, hp, LANES), per_b),
                  pl.BlockSpec((1, 1, KV_LORA), per_b), pl.BlockSpec((1, 1, LANES), per_b)] + lat_specs + kr_specs,
        out_specs=pl.BlockSpec((1, hp, KV_LORA), per_b),
        scratch_shapes=[pltpu.VMEM((hp, 1), F32), pltpu.VMEM((hp, 1), F32), pltpu.VMEM((hp, KV_LORA), F32)],
    )
    return pl.pallas_call(
        functools.partial(_mla_decode_kernel, pages=pages, layer=layer, n_layers=n_layers),
        grid_spec=grid_spec,
        out_shape=jax.ShapeDtypeStruct((b, hp, KV_LORA), F32),
        compiler_params=_cparams(2),
        name="mla_decode",
    )(page_table.reshape(-1), qlat, qr, lat_new, kr_new, *([cache_lat] * pages), *([cache_kr] * pages))


def _proj_rope_kernel(x_ref, w_ref, c_ref, slo_ref, shi_ref, *out_refs, segs):
    xb = x_ref[...].astype(BF16)
    for c0, n, rope, oi, o0 in segs:
        r = _dot(xb, w_ref[:, c0:c0 + n])
        if rope:
            c, slo, shi = c_ref[...], slo_ref[...], shi_ref[...]
            r = jnp.concatenate([_rope_apply(r[:, a:a + HEAD_DIM], c, slo, shi, HEAD_DIM // 2)
                                 for a in range(0, n, HEAD_DIM)], axis=-1)
        out_refs[oi][:, o0:o0 + n] = r.astype(out_refs[oi].dtype)


def _proj_rope(x, w, tabs, n_tab_blocks, segs, outs):
    m, d = x.shape
    tm = _row_tile(m)
    row = lambda i: (i, 0)
    tab = lambda i: (i % n_tab_blocks, 0)
    return pl.pallas_call(
        functools.partial(_proj_rope_kernel, segs=segs),
        grid=(m // tm,),
        in_specs=[pl.BlockSpec((tm, d), row), pl.BlockSpec(w.shape, lambda i: (0, 0)),
                  pl.BlockSpec((tm, LANES), tab), pl.BlockSpec((tm, LANES), tab), pl.BlockSpec((tm, LANES), tab)],
        out_specs=[pl.BlockSpec((tm, n), row) for n, _ in outs],
        out_shape=[jax.ShapeDtypeStruct((m, n), dt) for n, dt in outs],
        compiler_params=_cparams(1),
        name="proj_rope",
    )(x, w, *tabs)


def _band_attn_kernel(q_ref, kp_ref, kc_ref, vp_ref, vc_ref, o_ref, lse_ref):
    i = pl.program_id(1)
    t = q_ref.shape[1]
    rows = lax.broadcasted_iota(jnp.int32, (t, t), 0)
    cols = lax.broadcasted_iota(jnp.int32, (t, t), 1)
    mask_cur = cols <= rows
    mask_prev = jnp.logical_and(cols >= rows, i > 0)
    for h in range(DIL_KV_HEADS):
        sl = slice(h * HEAD_DIM, (h + 1) * HEAD_DIM)
        q = q_ref[0, :, sl]
        s_c = jnp.where(mask_cur, _dot_t(q, kc_ref[0, :, sl]) * HEAD_SCALE, NEG_INF)
        s_p = jnp.where(mask_prev, _dot_t(q, kp_ref[0, :, sl]) * HEAD_SCALE, NEG_INF)
        m = jnp.maximum(jnp.max(s_c, -1, keepdims=True), jnp.max(s_p, -1, keepdims=True))
        e_c = jnp.exp(s_c - m)
        e_p = jnp.exp(s_p - m)
        l = jnp.sum(e_c, -1, keepdims=True) + jnp.sum(e_p, -1, keepdims=True)
        o = _dot(e_c.astype(BF16), vc_ref[0, :, sl]) + _dot(e_p.astype(BF16), vp_ref[0, :, sl])
        o_ref[0, :, sl] = o / l
        lse_ref[0, :, sl] = jnp.broadcast_to(m + jnp.log(l), (t, HEAD_DIM))


def _band_attn(q, k, v):
    s, l, w = q.shape
    t = DIL_KEYS
    cur = lambda a, i: (a, i, 0)
    prev = lambda a, i: (a, jnp.maximum(i - 1, 0), 0)
    blk = (1, t, w)
    return pl.pallas_call(
        _band_attn_kernel,
        grid=(s, l // t),
        in_specs=[pl.BlockSpec(blk, cur), pl.BlockSpec(blk, prev), pl.BlockSpec(blk, cur),
                  pl.BlockSpec(blk, prev), pl.BlockSpec(blk, cur)],
        out_specs=[pl.BlockSpec(blk, cur), pl.BlockSpec(blk, cur)],
        out_shape=[jax.ShapeDtypeStruct((s, l, w), F32), jax.ShapeDtypeStruct((s, l, w), F32)],
        compiler_params=_cparams(2),
        name="band_attn",
    )(q, k, k, v, v)


def _dil_sample_kernel(q_ref, kn_ref, vn_ref, *refs, bb):
    k_refs, v_refs, o_ref = refs[:N_DIL_GROUPS], refs[N_DIL_GROUPS:2 * N_DIL_GROUPS], refs[2 * N_DIL_GROUPS]
    for j in range(bb):
        outs, lses = [], []
        for g in range(N_DIL_GROUPS):
            q_row = q_ref[pl.ds(j, 1), g * DIL_KV_DIM:(g + 1) * DIL_KV_DIM]
            k_heads, v_heads = [], []
            for h in range(DIL_KV_HEADS):
                sl = slice(h * HEAD_DIM, (h + 1) * HEAD_DIM)
                k_new = jnp.broadcast_to(kn_ref[pl.ds(j, 1), sl], (8, HEAD_DIM))
                v_new = jnp.broadcast_to(vn_ref[pl.ds(j, 1), sl], (8, HEAD_DIM))
                k_heads.append(jnp.concatenate([k_refs[g][j, :, h, :], k_new], 0).astype(BF16))
                v_heads.append(jnp.concatenate([v_refs[g][j, :, h, :], v_new], 0).astype(BF16))
            o, lse = _single_query_attend(q_row, k_heads, v_heads, DIL_KEYS + 1, False)
            outs.append(o)
            lses.append(lse)
        mx = functools.reduce(jnp.maximum, lses)
        ws = [jnp.exp(x - mx) for x in lses]
        mix = sum(o * w for o, w in zip(outs, ws)) / sum(ws)
        o_ref[pl.ds(j, 1), :] = _collapse_heads(mix).astype(o_ref.dtype)


def _dil_sample(q, k_new, v_new, cache_k, cache_v):
    b, wbuf = cache_k.shape[:2]
    bb = 8
    views, specs = [], []
    for window, d in DIL_PATTERNS:
        assert wbuf % (d * DIL_KEYS) == 0 and wbuf >= window
        last = wbuf // d // DIL_KEYS - 1
        specs.append(pl.BlockSpec((bb, DIL_KEYS, None, DIL_KV_HEADS, HEAD_DIM),
                                  functools.partial(lambda i, last_: (i, last_, 0, 0, 0), last_=last)))
        views.append(lambda c, d_=d: c.reshape(b, wbuf // d_, d_, DIL_KV_HEADS, HEAD_DIM))
    row = lambda i: (i, 0)
    return pl.pallas_call(
        functools.partial(_dil_sample_kernel, bb=bb),
        grid=(b // bb,),
        in_specs=[pl.BlockSpec((bb, MIX_HEADS * HEAD_DIM), row), pl.BlockSpec((bb, DIL_KV_DIM), row),
                  pl.BlockSpec((bb, DIL_KV_DIM), row)] + specs + specs,
        out_specs=pl.BlockSpec((bb, DIL_KV_DIM), row),
        out_shape=jax.ShapeDtypeStruct((b, DIL_KV_DIM), BF16),
        compiler_params=_cparams(1),
        name="dil_sample",
    )(q, k_new, v_new, *[f(cache_k) for f in views], *[f(cache_v) for f in views])


SWA_SHIFT_BATCH = 2
SWA_SHIFT_BUFFERS = 3


def _swa_shift_kernel(c_hbm, new_hbm, o_hbm, buf, in_sem, out_sem, new_sem):
    b, w = c_hbm.shape[:2]
    nbuf, bb = buf.shape[:2]
    n = b // bb

    def copy_in(c, slot):
        return pltpu.make_async_copy(c_hbm.at[pl.ds(c * bb, bb)], buf.at[slot], in_sem.at[slot])

    def copy_out(c, slot):
        return pltpu.make_async_copy(buf.at[slot, :, pl.ds(1, w - 1)],
                                     o_hbm.at[pl.ds(c * bb, bb), pl.ds(0, w - 1)], out_sem.at[slot])

    new_copy = pltpu.make_async_copy(new_hbm, o_hbm.at[:, pl.ds(w - 1, 1)], new_sem)
    new_copy.start()
    for c in range(min(nbuf - 1, n)):
        copy_in(c, c).start()

    def body(c, carry):
        slot = c % nbuf
        copy_in(c, slot).wait()
        copy_out(c, slot).start()

        @pl.when(c >= 1)
        def _():
            copy_out(c - 1, (c - 1) % nbuf).wait()

        @pl.when(c + nbuf - 1 < n)
        def _():
            copy_in(c + nbuf - 1, (c + nbuf - 1) % nbuf).start()

        return carry

    lax.fori_loop(0, n, body, 0)
    copy_out(n - 1, (n - 1) % nbuf).wait()
    new_copy.wait()


def _swa_shift(cache, new):
    b = cache.shape[0]
    bb = SWA_SHIFT_BATCH
    assert b % bb == 0
    any_spec = pl.BlockSpec(memory_space=pl.ANY)
    return pl.pallas_call(
        _swa_shift_kernel,
        in_specs=[any_spec, any_spec],
        out_specs=any_spec,
        out_shape=jax.ShapeDtypeStruct(cache.shape, cache.dtype),
        scratch_shapes=[pltpu.VMEM((SWA_SHIFT_BUFFERS, bb) + cache.shape[1:], cache.dtype),
                        pltpu.SemaphoreType.DMA((SWA_SHIFT_BUFFERS,)), pltpu.SemaphoreType.DMA((SWA_SHIFT_BUFFERS,)),
                        pltpu.SemaphoreType.DMA(())],
        compiler_params=pltpu.CompilerParams(vmem_limit_bytes=VMEM_LIMIT),
        name="swa_shift",
    )(cache, new)


ROUTE_GROUP_COL = 0
ROUTE_EXPERT_COL = 8


def _layer_norm(s, g, b):
    mu = jnp.mean(s, -1, keepdims=True)
    d = s - mu
    var = jnp.mean(d * d, -1, keepdims=True)
    return d * lax.rsqrt(var + LN_EPS) * g + b


def _route(x, w, bias):
    logits = _dot(x.astype(BF16), w) + bias
    col = lax.broadcasted_iota(jnp.int32, logits.shape, 1).astype(F32)
    big = float(LANES)
    is_g = col < N_EXPERT_GROUPS
    gl = jnp.where(is_g, logits, NEG_INF)
    gmax = jnp.max(gl, -1, keepdims=True)
    g_sel = jnp.min(jnp.where(gl == gmax, col, big), -1, keepdims=True)
    g_w = 1.0 / jnp.sum(jnp.where(is_g, jnp.exp(gl - gmax), 0.0), -1, keepdims=True)
    lo = ROUTE_EXPERT_COL + EXPERTS_PER_GROUP * g_sel
    el = jnp.where(jnp.logical_and(col >= lo, col < lo + EXPERTS_PER_GROUP), logits, NEG_INF)
    m1 = jnp.max(el, -1, keepdims=True)
    i1 = jnp.min(jnp.where(el == m1, col, big), -1, keepdims=True)
    el2 = jnp.where(col == i1, NEG_INF, el)
    m2 = jnp.max(el2, -1, keepdims=True)
    i2 = jnp.min(jnp.where(el2 == m2, col, big), -1, keepdims=True)
    e2 = jnp.exp(m2 - m1)
    w1 = g_w / (1.0 + e2)
    w2 = g_w * e2 / (1.0 + e2)
    out = jnp.where(col == 0, i1 - ROUTE_EXPERT_COL, 0.0)
    out = jnp.where(col == 1, i2 - ROUTE_EXPERT_COL, out)
    out = jnp.where(col == 2, w1, out)
    return jnp.where(col == 3, w2, out)


def _attn_out_tail(delta, h_ref, g_ref, b_ref, wr_ref, rb_ref, h1_ref, route_ref):
    h1 = _layer_norm(DEEPNORM_ALPHA * h_ref[...] + delta, g_ref[...], b_ref[...])
    h1_ref[...] = h1
    route_ref[...] = _route(h1, wr_ref[...], rb_ref[...])


def _attn_out_mla_kernel(mix_ref, mem_ref, w_ref, *tail):
    n_mix = mix_ref.shape[1]
    delta = _dot(mix_ref[...], w_ref[0:n_mix, :]) + _dot(mem_ref[...], w_ref[n_mix:, :])
    _attn_out_tail(delta, *tail)


def _attn_out_dil_kernel(o0_ref, o1_ref, o2_ref, l0_ref, l1_ref, l2_ref, mem_ref, w_ref, *tail):
    l0, l1, l2 = l0_ref[...], l1_ref[...], l2_ref[...]
    mx = jnp.maximum(jnp.maximum(l0, l1), l2)
    e0, e1, e2 = jnp.exp(l0 - mx), jnp.exp(l1 - mx), jnp.exp(l2 - mx)
    mix = (o0_ref[...] * e0 + o1_ref[...] * e1 + o2_ref[...] * e2) / (e0 + e1 + e2)
    n_mix = mix.shape[1]
    delta = _dot(mix.astype(BF16), w_ref[0:n_mix, :]) + _dot(mem_ref[...], w_ref[n_mix:, :])
    _attn_out_tail(delta, *tail)


def _attn_out(kernel_fn, parts, w, h, ln_g, ln_b, w_route, r_bias, name):
    m = h.shape[0]
    tm = min(_row_tile(m), 256)
    row = lambda i: (i, 0)
    fixed = lambda i: (0, 0)
    full = lambda a: pl.BlockSpec(a.shape, fixed)
    return pl.pallas_call(
        kernel_fn,
        grid=(m // tm,),
        in_specs=[pl.BlockSpec((tm, p.shape[1]), row) for p in parts]
        + [full(w), pl.BlockSpec((tm, D_MODEL), row), full(ln_g), full(ln_b), full(w_route), full(r_bias)],
        out_specs=[pl.BlockSpec((tm, D_MODEL), row), pl.BlockSpec((tm, LANES), row)],
        out_shape=[jax.ShapeDtypeStruct((m, D_MODEL), F32), jax.ShapeDtypeStruct((m, LANES), F32)],
        compiler_params=_cparams(1),
        name=name,
    )(*parts, w, h, ln_g, ln_b, w_route, r_bias)


def _moe_num_tiles(n_tokens):
    worst_rows = 2 * n_tokens + N_EXPERTS * (MOE_TM - 1)
    return -(-worst_rows // MOE_TM)


def _moe_dispatch(route):
    n = route.shape[0]
    ids = route[:, 0:2].astype(jnp.int32).reshape(-1)
    gates = route[:, 2:4].reshape(-1)
    onehot = (ids[:, None] == jnp.arange(N_EXPERTS, dtype=jnp.int32)[None, :]).astype(jnp.int32)
    counts = jnp.sum(onehot, 0)
    rank = jnp.sum((jnp.cumsum(onehot, 0) - onehot) * onehot, 1)
    padded = (counts + MOE_TM - 1) // MOE_TM * MOE_TM
    ends = jnp.cumsum(padded)
    pos = (ends - padded)[ids] + rank
    n_tiles = _moe_num_tiles(n)
    rows = n_tiles * MOE_TM
    row_token = jnp.zeros((rows,), jnp.int32).at[pos].set(jnp.arange(2 * n, dtype=jnp.int32) // 2)
    row_gate = jnp.zeros((rows,), F32).at[pos].set(gates)
    tile_start = jnp.arange(n_tiles, dtype=jnp.int32) * MOE_TM
    tile_expert = jnp.minimum(jnp.searchsorted(ends, tile_start, side="right"), N_EXPERTS - 1).astype(jnp.int32)
    n_valid = (ends[-1] // MOE_TM).astype(jnp.int32).reshape(1)
    return tile_expert, n_valid, row_token, row_gate.reshape(rows, 1), pos.astype(jnp.int32)


def _row_copy(src_hbm, src_row, dst_buf, dst_row, sem):
    return pltpu.make_async_copy(src_hbm.at[pl.ds(src_row, 1)], dst_buf.at[pl.ds(dst_row, 1)], sem)


def _moe_ffn_kernel(te_ref, nv_ref, rt_ref, x_hbm, gate_ref, wg_ref, wu_ref, wd_ref, y_ref,
                    xbuf, sem, wg_b, wu_b, wd_b):
    i = pl.program_id(0)
    n_valid = nv_ref[0]
    tm = xbuf.shape[1]

    def gather(tile, slot):
        def body(k, carry):
            _row_copy(x_hbm, rt_ref[tile * tm + k], xbuf.at[slot], k, sem.at[slot]).start()
            return carry
        lax.fori_loop(0, tm, body, 0)

    def gather_wait(slot):
        def body(k, carry):
            _row_copy(x_hbm, 0, xbuf.at[slot], k, sem.at[slot]).wait()
            return carry
        lax.fori_loop(0, tm, body, 0)

    @pl.when(jnp.logical_and(i == 0, n_valid > 0))
    def _():
        gather(0, 0)

    @pl.when(i + 1 < n_valid)
    def _():
        gather(i + 1, (i + 1) % 2)

    @pl.when(i < n_valid)
    def _():
        new_expert = jnp.logical_or(i == 0, te_ref[i] != te_ref[jnp.maximum(i - 1, 0)])

        @pl.when(new_expert)
        def _():
            wg_b[...] = wg_ref[0].astype(BF16)
            wu_b[...] = wu_ref[0].astype(BF16)
            wd_b[...] = wd_ref[0].astype(BF16)

        slot = i % 2
        gather_wait(slot)
        xb = xbuf[slot].astype(BF16)
        g = _dot(xb, wg_b[...])
        u = _dot(xb, wu_b[...])
        hid = g / (1.0 + jnp.exp(-g)) * u * gate_ref[...]
        y_ref[...] = _dot(hid.astype(BF16), wd_b[...])

    @pl.when(i >= n_valid)
    def _():
        y_ref[...] = jnp.zeros_like(y_ref)


def _moe_ffn(x, w_g, w_u, w_d, layer, tile_expert, n_valid, row_token, row_gate):
    d = x.shape[1]
    hdim = w_g.shape[3]
    rows = row_token.shape[0]
    tm = MOE_TM
    n_tiles = rows // tm
    live = lambda i, te, nv, rt: jnp.minimum(i, jnp.maximum(nv[0] - 1, 0))
    grid_spec = pltpu.PrefetchScalarGridSpec(
        num_scalar_prefetch=3,
        grid=(n_tiles,),
        in_specs=[pl.BlockSpec(memory_space=pl.ANY),
                  pl.BlockSpec((tm, 1), lambda i, te, nv, rt: (live(i, te, nv, rt), 0)),
                  pl.BlockSpec((None, 1, d, hdim), lambda i, te, nv, rt: (layer, te[live(i, te, nv, rt)], 0, 0)),
                  pl.BlockSpec((None, 1, d, hdim), lambda i, te, nv, rt: (layer, te[live(i, te, nv, rt)], 0, 0)),
                  pl.BlockSpec((None, 1, hdim, d), lambda i, te, nv, rt: (layer, te[live(i, te, nv, rt)], 0, 0))],
        out_specs=pl.BlockSpec((tm, d), lambda i, te, nv, rt: (i, 0)),
        scratch_shapes=[pltpu.VMEM((2, tm, d), F32), pltpu.SemaphoreType.DMA((2,)),
                        pltpu.VMEM((d, hdim), BF16), pltpu.VMEM((d, hdim), BF16), pltpu.VMEM((hdim, d), BF16)],
    )
    return pl.pallas_call(
        _moe_ffn_kernel,
        grid_spec=grid_spec,
        out_shape=jax.ShapeDtypeStruct((rows, d), F32),
        compiler_params=_cparams(1),
        name="moe_ffn",
    )(tile_expert, n_valid, row_token, x, row_gate, w_g, w_u, w_d)


def _moe_combine_kernel(pos_ref, h_ref, y_hbm, g_ref, b_ref, o_ref, ybuf, sem, *, tok0):
    i = pl.program_id(0)
    n_steps = pl.num_programs(0)
    tm = h_ref.shape[0]

    def gather(step, slot):
        def body(k, carry):
            base = 2 * (tok0 + step * tm + k)
            _row_copy(y_hbm, pos_ref[base], ybuf.at[slot, 0], k, sem.at[slot]).start()
            _row_copy(y_hbm, pos_ref[base + 1], ybuf.at[slot, 1], k, sem.at[slot]).start()
            return carry
        lax.fori_loop(0, tm, body, 0)

    def gather_wait(slot):
        def body(k, carry):
            _row_copy(y_hbm, 0, ybuf.at[slot, 0], k, sem.at[slot]).wait()
            _row_copy(y_hbm, 0, ybuf.at[slot, 1], k, sem.at[slot]).wait()
            return carry
        lax.fori_loop(0, tm, body, 0)

    @pl.when(i == 0)
    def _():
        gather(0, 0)

    @pl.when(i + 1 < n_steps)
    def _():
        gather(i + 1, (i + 1) % 2)

    slot = i % 2
    gather_wait(slot)
    s = DEEPNORM_ALPHA * h_ref[...] + (ybuf[slot, 0] + ybuf[slot, 1])
    o_ref[...] = _layer_norm(s, g_ref[...], b_ref[...])


def _moe_combine(pos, h1, y_sorted, ln_g, ln_b, tok0, n_rows):
    d = h1.shape[1]
    tm = min(_row_tile(n_rows), 256)
    assert tok0 % tm == 0
    blk0 = tok0 // tm
    grid_spec = pltpu.PrefetchScalarGridSpec(
        num_scalar_prefetch=1,
        grid=(n_rows // tm,),
        in_specs=[pl.BlockSpec((tm, d), lambda i, p: (blk0 + i, 0)),
                  pl.BlockSpec(memory_space=pl.ANY),
                  pl.BlockSpec((1, d), lambda i, p: (0, 0)), pl.BlockSpec((1, d), lambda i, p: (0, 0))],
        out_specs=pl.BlockSpec((tm, d), lambda i, p: (i, 0)),
        scratch_shapes=[pltpu.VMEM((2, 2, tm, d), F32), pltpu.SemaphoreType.DMA((2,))],
    )
    return pl.pallas_call(
        functools.partial(_moe_combine_kernel, tok0=tok0),
        grid_spec=grid_spec,
        out_shape=jax.ShapeDtypeStruct((n_rows, d), F32),
        compiler_params=_cparams(1),
        name="moe_combine",
    )(pos, h1, y_sorted, ln_g, ln_b)


def _router_weights(w_rg, b_rg, w_re, b_re):
    d = w_rg.shape[0]
    w = jnp.zeros((d, LANES), F32)
    w = w.at[:, ROUTE_GROUP_COL:ROUTE_GROUP_COL + N_EXPERT_GROUPS].set(w_rg)
    w = w.at[:, ROUTE_EXPERT_COL:ROUTE_EXPERT_COL + N_EXPERTS].set(w_re)
    bias = jnp.zeros((1, LANES), F32)
    bias = bias.at[0, ROUTE_GROUP_COL:ROUTE_GROUP_COL + N_EXPERT_GROUPS].set(b_rg)
    bias = bias.at[0, ROUTE_EXPERT_COL:ROUTE_EXPERT_COL + N_EXPERTS].set(b_re)
    return w.astype(BF16), bias


def _moe_block(h1_p, route_p, h1_s, route_s, w_g, w_u, w_d, layer, ln_g, ln_b):
    n_p, n_s = h1_p.shape[0], h1_s.shape[0]
    h1 = jnp.concatenate([h1_p, h1_s], axis=0)
    route = jnp.concatenate([route_p, route_s], axis=0)
    tile_expert, n_valid, row_token, row_gate, pos = _moe_dispatch(route)
    y_sorted = _moe_ffn(h1, w_g, w_u, w_d, layer, tile_expert, n_valid, row_token, row_gate)
    out_p = _moe_combine(pos, h1, y_sorted, ln_g, ln_b, 0, n_p)
    out_s = _moe_combine(pos, h1, y_sorted, ln_g, ln_b, n_p, n_s)
    return out_p, out_s


def kernel(x_prompt, x_sample, cache_mla_latent, cache_mla_krope, cache_swa_k, cache_swa_v, cache_mem_k, cache_mem_v, page_table, mem_prompt, w_in_a, q_norm_a, kv_norm_a, w_q_up_a, w_kv_up_a, w_out_a, w_in_b, w_kv_shared, w_out_b, w_mem_kv, ln1_g, ln1_b, ln2_g, ln2_b, w_router_group, b_router_group, w_router_expert, b_router_expert, w_exp_gate, w_exp_up, w_exp_down):
    bp, tp, d = x_prompt.shape
    bs, ts, _ = x_sample.shape
    assert ts == 1 and w_in_a.shape[0] == 1 and w_in_b.shape[0] == 1
    n_p, n_s = bp * tp, bs * ts
    n_mem = mem_prompt.shape[1]
    wbuf = cache_swa_k.shape[1]
    past_len = page_table.shape[1] * PAGE_SIZE
    assert wbuf == DIL_PATTERNS[-1][0]

    pos_p = jnp.arange(tp, dtype=jnp.int32)
    pos_s = jnp.full((n_s,), past_len, jnp.int32)
    tm_p = _row_tile(n_p)
    assert tp % tm_p == 0
    tab_blocks_p = tp // tm_p
    tabs_r_p = _rope_tables(pos_p, ROPE_DIM, LANES)
    tabs_r_s = _rope_tables(pos_s, ROPE_DIM, LANES)
    tabs_h_p = _rope_tables(pos_p, HEAD_DIM, LANES)
    tabs_h_s = _rope_tables(pos_s, HEAD_DIM, LANES)

    hp = x_prompt.reshape(n_p, d)
    hs = x_sample.reshape(n_s, d)
    ln = lambda a, l: a[l].reshape(1, d)

    mem2d = mem_prompt.reshape(bp * n_mem, d)
    mkv = [_matmul(mem2d, w_mem_kv[l].astype(BF16)).reshape(bp, n_mem, 2 * MEM_DIM) for l in range(DEPTH)]

    w_in = w_in_a[0]
    o_kr = Q_LORA + KV_LORA
    w_in = jnp.concatenate([w_in[:, :o_kr + ROPE_DIM], jnp.zeros((d, LANES - ROPE_DIM), F32),
                            w_in[:, o_kr + ROPE_DIM:]], axis=1).astype(BF16)
    w_q = jnp.concatenate([w_q_up_a[0], jnp.zeros((Q_LORA, MIX_HEADS, MLA_QK - NOPE_DIM - ROPE_DIM), F32)], axis=-1)
    w_q = w_q.reshape(Q_LORA, MIX_HEADS * MLA_QK).astype(BF16)
    w_kv = w_kv_up_a[0].reshape(KV_LORA, MIX_HEADS * (NOPE_DIM + V_DIM)).astype(BF16)
    w_uk_t = jnp.transpose(w_kv_up_a[0][..., :NOPE_DIM], (1, 2, 0)).astype(BF16)
    w_uv = jnp.transpose(w_kv_up_a[0][..., NOPE_DIM:], (1, 0, 2)).astype(BF16)
    q_g, kv_g = q_norm_a[0].reshape(1, Q_LORA), kv_norm_a[0].reshape(1, KV_LORA)

    cq_p, lat_p, kr_p, qm_p = _mla_in_proj(hp, w_in, q_g, kv_g, tabs_r_p, tab_blocks_p)
    cq_s, lat_s, kr_s, qm_s = _mla_in_proj(hs, w_in, q_g, kv_g, tabs_r_s, 1)
    q_p = _q_up(cq_p, w_q, tabs_r_p, tab_blocks_p)
    q_s = _q_up(cq_s, w_q, tabs_r_s, 1)

    k_full, v_full = _kv_up(lat_p, kr_p, w_kv)
    mix_p = _mla_flash(q_p.reshape(bp, tp, -1), k_full.reshape(bp, tp, -1), v_full.reshape(bp, tp, -1)).reshape(n_p, -1)

    hpad = 16
    q_s3 = q_s.reshape(n_s, MIX_HEADS, MLA_QK)
    qlat = _head_matmul(jnp.transpose(q_s3[:, :, :NOPE_DIM], (1, 0, 2)), w_uk_t, BF16)
    qlat = jnp.pad(jnp.transpose(qlat, (1, 0, 2)), ((0, 0), (0, hpad - MIX_HEADS), (0, 0)))
    qr = jnp.pad(q_s3[:, :, NOPE_DIM:], ((0, 0), (0, hpad - MIX_HEADS), (0, 0)))
    o_lat = _mla_decode(qlat, qr, lat_s.reshape(n_s, 1, KV_LORA), kr_s.reshape(n_s, 1, LANES),
                        cache_mla_latent, cache_mla_krope, page_table, 0)
    mix_s = _head_matmul(jnp.transpose(o_lat[:, :MIX_HEADS], (1, 0, 2)), w_uv, BF16)
    mix_s = jnp.transpose(mix_s, (1, 0, 2)).reshape(n_s, MIX_HEADS * V_DIM)

    mem_p = _mem_attn_prompt(qm_p.reshape(bp, tp, MEM_DIM), mkv[0]).reshape(n_p, MEM_DIM)
    mem_s = _mem_attn_sample(qm_s, cache_mem_k, cache_mem_v, 0)

    router = [_router_weights(w_router_group[l], b_router_group[l], w_router_expert[l], b_router_expert[l])
              for l in range(DEPTH)]
    w_out = w_out_a[0].astype(BF16)
    h1_p, route_p = _attn_out(_attn_out_mla_kernel, [mix_p, mem_p], w_out, hp, ln(ln1_g, 0), ln(ln1_b, 0),
                              *router[0], name="attn_out_mla")
    h1_s, route_s = _attn_out(_attn_out_mla_kernel, [mix_s, mem_s], w_out, hs, ln(ln1_g, 0), ln(ln1_b, 0),
                              *router[0], name="attn_out_mla")
    hp, hs = _moe_block(h1_p, route_p, h1_s, route_s, w_exp_gate, w_exp_up, w_exp_down, 0, ln(ln2_g, 0), ln(ln2_b, 0))

    n_q = MIX_HEADS * HEAD_DIM
    w_b = w_in_b[0].astype(BF16)
    w_kvs = w_kv_shared.astype(BF16)
    q_segs = tuple((g * DIL_KV_DIM, DIL_KV_DIM, True, 0, g * DIL_KV_DIM) for g in range(N_DIL_GROUPS))
    q_segs += ((n_q, MEM_DIM, False, 1, 0),)
    kv_segs = ((0, DIL_KV_DIM, True, 0, 0), (DIL_KV_DIM, DIL_KV_DIM, False, 1, 0))
    qd_p, qm_p = _proj_rope(hp, w_b, tabs_h_p, tab_blocks_p, q_segs, [(n_q, BF16), (MEM_DIM, BF16)])
    qd_s, qm_s = _proj_rope(hs, w_b, tabs_h_s, 1, q_segs, [(n_q, F32), (MEM_DIM, BF16)])
    k_p, v_p = _proj_rope(hp, w_kvs, tabs_h_p, tab_blocks_p, kv_segs, [(DIL_KV_DIM, F32), (DIL_KV_DIM, F32)])
    k_s, v_s = _proj_rope(hs, w_kvs, tabs_h_s, 1, kv_segs, [(DIL_KV_DIM, F32), (DIL_KV_DIM, F32)])

    k_pb, v_pb = k_p.astype(BF16), v_p.astype(BF16)
    outs, lses = [], []
    for g, (_, dil) in enumerate(DIL_PATTERNS):
        def split(a):
            a = a.reshape(bp, tp // dil, dil, DIL_KV_DIM)
            return jnp.transpose(a, (0, 2, 1, 3)).reshape(bp * dil, tp // dil, DIL_KV_DIM)

        def merge(a):
            a = a.reshape(bp, dil, tp // dil, DIL_KV_DIM)
            return jnp.transpose(a, (0, 2, 1, 3)).reshape(n_p, DIL_KV_DIM)

        o_g, lse_g = _band_attn(split(qd_p[:, g * DIL_KV_DIM:(g + 1) * DIL_KV_DIM]), split(k_pb), split(v_pb))
        outs.append(merge(o_g))
        lses.append(merge(lse_g))

    mix_s = _dil_sample(qd_s, k_s, v_s, cache_swa_k, cache_swa_v)
    mem_p = _mem_attn_prompt(qm_p.reshape(bp, tp, MEM_DIM), mkv[1]).reshape(n_p, MEM_DIM)
    mem_s = _mem_attn_sample(qm_s, cache_mem_k, cache_mem_v, 1)

    w_out = w_out_b[0].astype(BF16)
    h1_p, route_p = _attn_out(_attn_out_dil_kernel, outs + lses + [mem_p], w_out, hp, ln(ln1_g, 1), ln(ln1_b, 1),
                              *router[1], name="attn_out_dil")
    h1_s, route_s = _attn_out(_attn_out_mla_kernel, [mix_s, mem_s], w_out, hs, ln(ln1_g, 1), ln(ln1_b, 1),
                              *router[1], name="attn_out_mla")
    hp, hs = _moe_block(h1_p, route_p, h1_s, route_s, w_exp_gate, w_exp_up, w_exp_down, 1, ln(ln2_g, 1), ln(ln2_b, 1))

    kv4 = lambda a, n: a.reshape(n, -1, DIL_KV_HEADS, HEAD_DIM)
    new_k_s = _swa_shift(cache_swa_k, kv4(k_s, bs))
    new_v_s = _swa_shift(cache_swa_v, kv4(v_s, bs))
    keep_p = min(wbuf, tp)
    k_p4, v_p4 = kv4(k_p, bp), kv4(v_p, bp)
    mem_k = jnp.stack([m[:, :, :MEM_DIM].reshape(bp, n_mem, MEM_HEADS, HEAD_DIM) for m in mkv], axis=2)
    mem_v = jnp.stack([m[:, :, MEM_DIM:].reshape(bp, n_mem, MEM_HEADS, HEAD_DIM) for m in mkv], axis=2)
    return (hp.reshape(bp, tp, d), hs.reshape(bs, ts, d),
            lat_p.reshape(bp, tp, 1, KV_LORA), kr_p[:, :ROPE_DIM].reshape(bp, tp, 1, ROPE_DIM),
            lat_s.reshape(bs, ts, 1, KV_LORA), kr_s[:, :ROPE_DIM].reshape(bs, ts, 1, ROPE_DIM),
            k_p4[:, tp - keep_p:], v_p4[:, tp - keep_p:], new_k_s, new_v_s, mem_k, mem_v)
```

```python
import functools

import jax
import jax.numpy as jnp
import numpy as np
from jax import lax
from jax.experimental import pallas as pl
from jax.experimental.pallas import tpu as pltpu

BF16 = jnp.bfloat16
F32 = jnp.float32

D_MODEL = 2048
DEPTH = 2
HEAD_DIM = 128
MIX_HEADS = 12
MEM_HEADS = 4
MEM_DIM = MEM_HEADS * HEAD_DIM
Q_LORA = 512
KV_LORA = 256
NOPE_DIM = 128
ROPE_DIM = 64
V_DIM = 128
PAGE_SIZE = 128
DIL_PATTERNS = ((128, 1), (512, 4), (2048, 16))
N_DIL_GROUPS = len(DIL_PATTERNS)
DIL_KV_HEADS = MIX_HEADS // N_DIL_GROUPS
DIL_KV_DIM = DIL_KV_HEADS * HEAD_DIM
DIL_KEYS = 128
N_EXPERT_GROUPS = 4
EXPERTS_PER_GROUP = 8
N_EXPERTS = N_EXPERT_GROUPS * EXPERTS_PER_GROUP
EXPERT_HIDDEN = 512
ROPE_THETA = 10000.0
LN_EPS = 1e-5
RMS_EPS = 1e-6
DEEPNORM_ALPHA = (2.0 * DEPTH) ** 0.25
MLA_SCALE = (NOPE_DIM + ROPE_DIM) ** -0.5
HEAD_SCALE = HEAD_DIM ** -0.5
MLA_EXP2_SCALE = MLA_SCALE * float(np.log2(np.e))

LANES = 128
MLA_QK = 2 * LANES
VMEM_LIMIT = 56 * 1024 * 1024
MOE_TM = 256
MLA_PAGES_PER_STEP = 16
NEG_INF = float("-inf")
ROW_CHUNKS = D_MODEL // LANES


def _cparams(n_grid):
    return pltpu.CompilerParams(dimension_semantics=("arbitrary",) * n_grid, vmem_limit_bytes=VMEM_LIMIT)


def _row_tile(m):
    for t in (512, 256, 128, 64, 32, 16, 8):
        if m % t == 0:
            return t
    raise ValueError(f"row count {m} not a multiple of 8")


def _load_token_rows(ref, n):
    return jnp.concatenate([ref[pl.ds(c, n, stride=ROW_CHUNKS), :] for c in range(ROW_CHUNKS)], axis=-1)


def _store_token_rows(ref, val):
    n = val.shape[0]
    for c in range(ROW_CHUNKS):
        ref[pl.ds(c, n, stride=ROW_CHUNKS), :] = val[:, c * LANES:(c + 1) * LANES]


def _dot(a, b):
    return jnp.dot(a, b, preferred_element_type=F32)


def _dot_t(a, b):
    return lax.dot_general(a, b, (((1,), (1,)), ((), ())), preferred_element_type=F32)


def _rope_tables(pos, dim, width):
    half = dim // 2
    inv_freq = ROPE_THETA ** (-jnp.arange(0, dim, 2, dtype=F32) / dim)
    ang = pos.astype(F32)[:, None] * inv_freq[None, :]
    cos, sin = jnp.cos(ang), jnp.sin(ang)
    z = jnp.zeros((pos.shape[0], width - dim), F32)
    zh = jnp.zeros_like(cos)
    c = jnp.concatenate([cos, cos, z], axis=-1)
    s_lo = jnp.concatenate([-sin, zh, z], axis=-1)
    s_hi = jnp.concatenate([zh, sin, z], axis=-1)
    return c, s_lo, s_hi


def _rope_apply(x, c, s_lo, s_hi, half):
    width = x.shape[-1]
    return x * c + pltpu.roll(x, width - half, 1) * s_lo + pltpu.roll(x, half, 1) * s_hi


def _mm_kernel(x_ref, w_ref, o_ref):
    o_ref[...] = _dot(x_ref[...].astype(BF16), w_ref[...]).astype(o_ref.dtype)


def _matmul(x, w, out_dtype=F32):
    m, k = x.shape
    n = w.shape[1]
    tm = _row_tile(m)
    return pl.pallas_call(
        _mm_kernel,
        grid=(m // tm,),
        in_specs=[pl.BlockSpec((tm, k), lambda i: (i, 0)), pl.BlockSpec((k, n), lambda i: (0, 0))],
        out_specs=pl.BlockSpec((tm, n), lambda i: (i, 0)),
        out_shape=jax.ShapeDtypeStruct((m, n), out_dtype),
        compiler_params=_cparams(1),
        name="matmul",
    )(x, w)


def _bmm_kernel(x_ref, w_ref, o_ref):
    o_ref[0] = _dot(x_ref[0].astype(BF16), w_ref[0]).astype(o_ref.dtype)


def _head_matmul(x, w, out_dtype):
    h, m, k = x.shape
    n = w.shape[2]
    return pl.pallas_call(
        _bmm_kernel,
        grid=(h,),
        in_specs=[pl.BlockSpec((1, m, k), lambda i: (i, 0, 0)), pl.BlockSpec((1, k, n), lambda i: (i, 0, 0))],
        out_specs=pl.BlockSpec((1, m, n), lambda i: (i, 0, 0)),
        out_shape=jax.ShapeDtypeStruct((h, m, n), out_dtype),
        compiler_params=_cparams(1),
        name="head_matmul",
    )(x, w)


def _mla_in_kernel(x_ref, w_ref, qg_ref, kvg_ref, c_ref, slo_ref, shi_ref, cq_ref, lat_ref, kr_ref, qm_ref):
    xb = x_ref[...].astype(BF16)
    c_q = _dot(xb, w_ref[:, 0:Q_LORA])
    cq_ref[...] = (c_q * lax.rsqrt(jnp.mean(c_q * c_q, -1, keepdims=True) + RMS_EPS) * qg_ref[...]).astype(cq_ref.dtype)
    o = Q_LORA
    c_kv = _dot(xb, w_ref[:, o:o + KV_LORA])
    lat_ref[...] = c_kv * lax.rsqrt(jnp.mean(c_kv * c_kv, -1, keepdims=True) + RMS_EPS) * kvg_ref[...]
    o += KV_LORA
    k_rope = _dot(xb, w_ref[:, o:o + LANES])
    kr_ref[...] = _rope_apply(k_rope, c_ref[...], slo_ref[...], shi_ref[...], ROPE_DIM // 2)
    o += LANES
    qm_ref[...] = _dot(xb, w_ref[:, o:o + MEM_DIM]).astype(qm_ref.dtype)


def _mla_in_proj(x, w, q_g, kv_g, tabs, n_tab_blocks):
    m, d = x.shape
    tm = _row_tile(m)
    n = w.shape[1]
    row = lambda i: (i, 0)
    fixed = lambda i: (0, 0)
    tab = lambda i: (i % n_tab_blocks, 0)
    return pl.pallas_call(
        _mla_in_kernel,
        grid=(m // tm,),
        in_specs=[pl.BlockSpec((tm, d), row), pl.BlockSpec((d, n), fixed),
                  pl.BlockSpec((1, Q_LORA), fixed), pl.BlockSpec((1, KV_LORA), fixed),
                  pl.BlockSpec((tm, LANES), tab), pl.BlockSpec((tm, LANES), tab), pl.BlockSpec((tm, LANES), tab)],
        out_specs=[pl.BlockSpec((tm, Q_LORA), row), pl.BlockSpec((tm, KV_LORA), row),
                   pl.BlockSpec((tm, LANES), row), pl.BlockSpec((tm, MEM_DIM), row)],
        out_shape=[jax.ShapeDtypeStruct((m, Q_LORA), BF16), jax.ShapeDtypeStruct((m, KV_LORA), F32),
                   jax.ShapeDtypeStruct((m, LANES), F32), jax.ShapeDtypeStruct((m, MEM_DIM), BF16)],
        compiler_params=_cparams(1),
        name="mla_in_proj",
    )(x, w, q_g, kv_g, *tabs)


def _q_up_kernel(cq_ref, w_ref, c_ref, slo_ref, shi_ref, q_ref):
    cq = cq_ref[...]
    c, slo, shi = c_ref[...], slo_ref[...], shi_ref[...]
    for h in range(MIX_HEADS):
        r = _dot(cq, w_ref[:, h * MLA_QK:(h + 1) * MLA_QK])
        q_ref[:, h * MLA_QK:h * MLA_QK + LANES] = r[:, :LANES].astype(q_ref.dtype)
        roped = _rope_apply(r[:, LANES:], c, slo, shi, ROPE_DIM // 2)
        q_ref[:, h * MLA_QK + LANES:(h + 1) * MLA_QK] = roped.astype(q_ref.dtype)


def _q_up(cq, w, tabs, n_tab_blocks):
    m, k = cq.shape
    tm = _row_tile(m)
    n = w.shape[1]
    row = lambda i: (i, 0)
    fixed = lambda i: (0, 0)
    tab = lambda i: (i % n_tab_blocks, 0)
    return pl.pallas_call(
        _q_up_kernel,
        grid=(m // tm,),
        in_specs=[pl.BlockSpec((tm, k), row), pl.BlockSpec((k, n), fixed),
                  pl.BlockSpec((tm, LANES), tab), pl.BlockSpec((tm, LANES), tab), pl.BlockSpec((tm, LANES), tab)],
        out_specs=pl.BlockSpec((tm, n), row),
        out_shape=jax.ShapeDtypeStruct((m, n), BF16),
        compiler_params=_cparams(1),
        name="mla_q_up",
    )(cq, w, *tabs)


def _kv_up_kernel(lat_ref, kr_ref, w_ref, k_ref, v_ref):
    lat = lat_ref[...].astype(BF16)
    kr = kr_ref[...].astype(BF16)
    for h in range(MIX_HEADS):
        r = _dot(lat, w_ref[:, h * 2 * LANES:(h + 1) * 2 * LANES])
        k_ref[:, h * MLA_QK:h * MLA_QK + LANES] = r[:, :LANES].astype(BF16)
        k_ref[:, h * MLA_QK + LANES:(h + 1) * MLA_QK] = kr
        v_ref[:, h * V_DIM:(h + 1) * V_DIM] = r[:, LANES:].astype(BF16)


def _kv_up(lat, kr, w):
    m = lat.shape[0]
    tm = _row_tile(m)
    row = lambda i: (i, 0)
    return pl.pallas_call(
        _kv_up_kernel,
        grid=(m // tm,),
        in_specs=[pl.BlockSpec((tm, KV_LORA), row), pl.BlockSpec((tm, LANES), row),
                  pl.BlockSpec(w.shape, lambda i: (0, 0))],
        out_specs=[pl.BlockSpec((tm, MIX_HEADS * MLA_QK), row), pl.BlockSpec((tm, MIX_HEADS * V_DIM), row)],
        out_shape=[jax.ShapeDtypeStruct((m, MIX_HEADS * MLA_QK), BF16),
                   jax.ShapeDtypeStruct((m, MIX_HEADS * V_DIM), BF16)],
        compiler_params=_cparams(1),
        name="mla_kv_up",
    )(lat, kr, w)


def _mla_flash_kernel(q_ref, k_ref, v_ref, o_ref, *, tq):
    i = pl.program_id(2)
    q = q_ref[0]

    def step(j, carry, masked):
        m, l, acc = carry
        start = pl.multiple_of(j * tq, tq)
        s = _dot_t(q, k_ref[0, pl.ds(start, tq), :])
        if masked:
            rows = lax.broadcasted_iota(jnp.int32, s.shape, 0)
            cols = lax.broadcasted_iota(jnp.int32, s.shape, 1)
            s = jnp.where(cols <= rows, s, NEG_INF)
        m_new = jnp.maximum(m, jnp.max(s, -1, keepdims=True))
        corr = jnp.exp2((m - m_new) * MLA_EXP2_SCALE)
        p = jnp.exp2((s - m_new) * MLA_EXP2_SCALE)
        l = l * corr + jnp.sum(p, -1, keepdims=True)
        acc = acc * corr + _dot(p.astype(BF16), v_ref[0, pl.ds(start, tq), :])
        return m_new, l, acc

    init = (jnp.full((tq, 1), NEG_INF, F32), jnp.zeros((tq, 1), F32), jnp.zeros((tq, V_DIM), F32))
    carry = lax.fori_loop(0, i, lambda j, c: step(j, c, False), init)
    _, l, acc = step(i, carry, True)
    o_ref[0] = (acc / l).astype(o_ref.dtype)


def _mla_flash(q, k, v):
    b, t, _ = q.shape
    tq = _row_tile(t)
    return pl.pallas_call(
        functools.partial(_mla_flash_kernel, tq=tq),
        grid=(b, MIX_HEADS, t // tq),
        in_specs=[pl.BlockSpec((1, tq, MLA_QK), lambda b_, h, i: (b_, i, h)),
                  pl.BlockSpec((1, t, MLA_QK), lambda b_, h, i: (b_, 0, h)),
                  pl.BlockSpec((1, t, V_DIM), lambda b_, h, i: (b_, 0, h))],
        out_specs=pl.BlockSpec((1, tq, V_DIM), lambda b_, h, i: (b_, i, h)),
        out_shape=jax.ShapeDtypeStruct((b, t, MIX_HEADS * V_DIM), BF16),
        compiler_params=_cparams(3),
        name="mla_flash",
    )(q, k, v)


def _mem_attn_kernel(q_ref, k_ref, v_ref, o_ref):
    for h in range(MEM_HEADS):
        sl = slice(h * HEAD_DIM, (h + 1) * HEAD_DIM)
        s = _dot_t(q_ref[0, :, sl], k_ref[0, :, sl].astype(BF16)) * HEAD_SCALE
        p = jnp.exp(s - jnp.max(s, -1, keepdims=True))
        p = p / jnp.sum(p, -1, keepdims=True)
        o_ref[0, :, sl] = _dot(p.astype(BF16), v_ref[0, :, sl].astype(BF16)).astype(o_ref.dtype)


def _mem_attn_prompt(q, mkv):
    b, t, _ = q.shape
    n_mem = mkv.shape[1]
    tq = _row_tile(t)
    return pl.pallas_call(
        _mem_attn_kernel,
        grid=(b, t // tq),
        in_specs=[pl.BlockSpec((1, tq, MEM_DIM), lambda b_, i: (b_, i, 0)),
                  pl.BlockSpec((1, n_mem, MEM_DIM), lambda b_, i: (b_, 0, 0)),
                  pl.BlockSpec((1, n_mem, MEM_DIM), lambda b_, i: (b_, 0, 1))],
        out_specs=pl.BlockSpec((1, tq, MEM_DIM), lambda b_, i: (b_, i, 0)),
        out_shape=jax.ShapeDtypeStruct((b, t, MEM_DIM), BF16),
        compiler_params=_cparams(2),
        name="mem_attn_prompt",
    )(q, mkv, mkv)


def _head_rows(width, rows=8):
    r = lax.broadcasted_iota(jnp.int32, (rows, width), 0)
    c = lax.broadcasted_iota(jnp.int32, (rows, width), 1)
    return (c // HEAD_DIM) == r


def _single_query_attend(q_row, k_heads, v_heads, valid_rows, normalise_first):
    n = k_heads[0].shape[0]
    row_id = lax.broadcasted_iota(jnp.int32, (8, HEAD_DIM), 0)
    s = None
    for h, k_h in enumerate(k_heads):
        q_h = jnp.broadcast_to(q_row[:, h * HEAD_DIM:(h + 1) * HEAD_DIM], (8, HEAD_DIM))
        s_h = _dot_t(k_h, jnp.where(row_id == h, q_h, 0.0).astype(BF16))
        s = s_h if s is None else s + s_h
    s = s * HEAD_SCALE
    if valid_rows < n:
        s = jnp.where(lax.broadcasted_iota(jnp.int32, s.shape, 0) < valid_rows, s, NEG_INF)
    m = jnp.max(s, 0, keepdims=True)
    e = jnp.exp(s - m)
    l = jnp.sum(e, 0, keepdims=True)
    eb = (e / l if normalise_first else e).astype(BF16)
    o = jnp.concatenate(
        [lax.dot_general(eb, v_h, (((0,), (0,)), ((), ())), preferred_element_type=F32) for v_h in v_heads], axis=-1)
    eye = lax.broadcasted_iota(jnp.int32, (8, 8), 0) == lax.broadcasted_iota(jnp.int32, (8, 8), 1)
    lse = jnp.sum(jnp.where(eye, jnp.broadcast_to(m + jnp.log(l), (8, 8)), 0.0), -1, keepdims=True)
    if normalise_first:
        return o, lse
    l_col = jnp.sum(jnp.where(eye, jnp.broadcast_to(l, (8, 8)), 0.0), -1, keepdims=True)
    return o / l_col, lse


def _collapse_heads(o):
    return jnp.sum(jnp.where(_head_rows(o.shape[1]), o, 0.0), 0, keepdims=True)


def _mem_attn_sample_kernel(q_ref, k_ref, v_ref, o_ref, *, bb):
    n_mem = k_ref.shape[1]
    for j in range(bb):
        k = [k_ref[j, :, h, :].astype(BF16) for h in range(MEM_HEADS)]
        v = [v_ref[j, :, h, :].astype(BF16) for h in range(MEM_HEADS)]
        o, _ = _single_query_attend(q_ref[pl.ds(j, 1), :].astype(F32), k, v, n_mem, True)
        o_ref[pl.ds(j, 1), :] = _collapse_heads(o).astype(o_ref.dtype)


def _mem_attn_sample(q, cache_k, cache_v, layer):
    b = q.shape[0]
    n_mem = cache_k.shape[1]
    bb = 8
    cspec = pl.BlockSpec((bb, n_mem, None, MEM_HEADS, HEAD_DIM), lambda i: (i, 0, layer, 0, 0))
    return pl.pallas_call(
        functools.partial(_mem_attn_sample_kernel, bb=bb),
        grid=(b // bb,),
        in_specs=[pl.BlockSpec((bb, MEM_DIM), lambda i: (i, 0)), cspec, cspec],
        out_specs=pl.BlockSpec((bb, MEM_DIM), lambda i: (i, 0)),
        out_shape=jax.ShapeDtypeStruct((b, MEM_DIM), BF16),
        compiler_params=_cparams(1),
        name="mem_attn_sample",
    )(q, cache_k, cache_v)


def _mla_decode_kernel(pt_ref, qlat_ref, qr_ref, latn_ref, krn_ref, lat_hbm, kr_hbm, o_ref,
                       lat_buf, kr_buf, sem, m_sc, l_sc, acc_sc, *, layer, n_layers):
    pages, page_rows = lat_buf.shape[1], lat_buf.shape[2]
    n_chunks = pl.num_programs(1)
    c = pl.program_id(1)
    step = pl.program_id(0) * n_chunks + c
    n_steps = pl.num_programs(0) * n_chunks

    def page_copies(page, slot, j):
        lat_cp = pltpu.make_async_copy(lat_hbm.at[pl.ds(pl.multiple_of(page * page_rows, page_rows), page_rows)],
                                       lat_buf.at[slot, j], sem.at[0, slot])
        kr_cp = pltpu.make_async_copy(kr_hbm.at[page, layer], kr_buf.at[slot, j], sem.at[1, slot])
        return lat_cp, kr_cp

    def fetch(at_step, slot):
        for j in range(pages):
            for cp in page_copies(pt_ref[at_step * pages + j], slot, j):
                cp.start()

    @pl.when(step == 0)
    def _():
        fetch(0, 0)

    @pl.when(step + 1 < n_steps)
    def _():
        fetch(step + 1, (step + 1) % 2)

    slot = step % 2
    for j in range(pages):
        for cp in page_copies(0, slot, j):
            cp.wait()

    qlat = qlat_ref[0]
    qr_full = qr_ref[0]
    qr = qr_full[:, :ROPE_DIM]

    def page_latent(j):
        stride = 2 * n_layers
        lo = lat_buf[slot, j, pl.ds(2 * layer, PAGE_SIZE, stride=stride), :]
        hi = lat_buf[slot, j, pl.ds(2 * layer + 1, PAGE_SIZE, stride=stride), :]
        return jnp.concatenate([lo, hi], axis=-1).astype(BF16)

    @pl.when(c == 0)
    def _():
        latn = latn_ref[0]
        s0 = (jnp.sum(qlat.astype(F32) * latn.astype(BF16).astype(F32), -1, keepdims=True)
              + jnp.sum(qr_full.astype(F32) * krn_ref[0].astype(BF16).astype(F32), -1, keepdims=True)) * MLA_SCALE
        m_sc[...] = s0
        l_sc[...] = jnp.ones_like(s0)
        acc_sc[...] = jnp.broadcast_to(latn.astype(BF16).astype(F32), acc_sc.shape)

    lats = [page_latent(j) for j in range(pages)]
    kr_all = jnp.concatenate([kr_buf[slot, j].astype(BF16) for j in range(pages)], axis=1)
    s_all = (_dot_t(qlat, jnp.concatenate(lats, axis=0)) + _dot(qr, kr_all)) * MLA_SCALE
    scores = [s_all[:, j * PAGE_SIZE:(j + 1) * PAGE_SIZE] for j in range(pages)]
    row_max = [jnp.max(sc, -1, keepdims=True) for sc in scores]
    ms = [m_sc[...]]
    for j in range(pages):
        ms.append(jnp.maximum(ms[-1], row_max[j]))
    ps = [jnp.exp(scores[j] - ms[j + 1]) for j in range(pages)]
    corrs = [jnp.exp(ms[j] - ms[j + 1]) for j in range(pages)]
    row_sum = [jnp.sum(p, -1, keepdims=True) for p in ps]
    pvs = [_dot(ps[j].astype(BF16), lats[j]) for j in range(pages)]
    l, acc = l_sc[...], acc_sc[...]
    for j in range(pages):
        l = l * corrs[j] + row_sum[j]
        acc = acc * corrs[j] + pvs[j]
    m_sc[...] = ms[-1]
    l_sc[...] = l
    acc_sc[...] = acc

    @pl.when(c == n_chunks - 1)
    def _():
        o_ref[0] = acc / l


def _mla_decode(qlat, qr, lat_new, kr_new, cache_lat, cache_kr, page_table, layer):
    b, hp, _ = qlat.shape
    n_pages = page_table.shape[1]
    n_phys, _, n_layers, _ = cache_lat.shape
    pages = min(MLA_PAGES_PER_STEP, n_pages)
    assert n_pages % pages == 0
    per_b = lambda i, c, pt: (i, 0, 0)
    page_rows = PAGE_SIZE * n_layers * (KV_LORA // LANES)
    cache_lat = cache_lat.reshape(n_phys * page_rows, LANES)
    cache_kr = jnp.transpose(cache_kr, (0, 2, 3, 1))
    any_spec = pl.BlockSpec(memory_space=pl.ANY)
    grid_spec = pltpu.PrefetchScalarGridSpec(
        num_scalar_prefetch=1,
        grid=(b, n_pages // pages),
        in_specs=[pl.BlockSpec((1, hp, KV_LORA), per_b), pl.BlockSpec((1, hp, LANES), per_b),
                  pl.BlockSpec((1, 1, KV_LORA), per_b), pl.BlockSpec((1, 1, LANES), per_b), any_spec, any_spec],
        out_specs=pl.BlockSpec((1, hp, KV_LORA), per_b),
        scratch_shapes=[pltpu.VMEM((2, pages, page_rows, LANES), F32), pltpu.VMEM((2, pages, ROPE_DIM, PAGE_SIZE), F32),
                        pltpu.SemaphoreType.DMA((2, 2)),
                        pltpu.VMEM((hp, 1), F32), pltpu.VMEM((hp, 1), F32), pltpu.VMEM((hp, KV_LORA), F32)],
    )
    return pl.pallas_call(
        functools.partial(_mla_decode_kernel, layer=layer, n_layers=n_layers),
        grid_spec=grid_spec,
        out_shape=jax.ShapeDtypeStruct((b, hp, KV_LORA), F32),
        compiler_params=_cparams(2),
        name="mla_decode",
    )(page_table.reshape(-1), qlat, qr, lat_new, kr_new, cache_lat, cache_kr)


def _proj_rope_kernel(x_ref, w_ref, c_ref, slo_ref, shi_ref, *out_refs, segs):
    xb = x_ref[...].astype(BF16)
    for c0, n, rope, oi, o0 in segs:
        r = _dot(xb, w_ref[:, c0:c0 + n])
        if rope:
            c, slo, shi = c_ref[...], slo_ref[...], shi_ref[...]
            r = jnp.concatenate([_rope_apply(r[:, a:a + HEAD_DIM], c, slo, shi, HEAD_DIM // 2)
                                 for a in range(0, n, HEAD_DIM)], axis=-1)
        out_refs[oi][:, o0:o0 + n] = r.astype(out_refs[oi].dtype)


def _proj_rope(x, w, tabs, n_tab_blocks, segs, outs):
    m, d = x.shape
    tm = _row_tile(m)
    row = lambda i: (i, 0)
    tab = lambda i: (i % n_tab_blocks, 0)
    return pl.pallas_call(
        functools.partial(_proj_rope_kernel, segs=segs),
        grid=(m // tm,),
        in_specs=[pl.BlockSpec((tm, d), row), pl.BlockSpec(w.shape, lambda i: (0, 0)),
                  pl.BlockSpec((tm, LANES), tab), pl.BlockSpec((tm, LANES), tab), pl.BlockSpec((tm, LANES), tab)],
        out_specs=[pl.BlockSpec((tm, n), row) for n, _ in outs],
        out_shape=[jax.ShapeDtypeStruct((m, n), dt) for n, dt in outs],
        compiler_params=_cparams(1),
        name="proj_rope",
    )(x, w, *tabs)


def _band_attn_kernel(q_ref, kp_ref, kc_ref, vp_ref, vc_ref, o_ref, lse_ref):
    i = pl.program_id(1)
    t = q_ref.shape[1]
    rows = lax.broadcasted_iota(jnp.int32, (t, t), 0)
    cols = lax.broadcasted_iota(jnp.int32, (t, t), 1)
    mask_cur = cols <= rows
    mask_prev = jnp.logical_and(cols >= rows, i > 0)
    for h in range(DIL_KV_HEADS):
        sl = slice(h * HEAD_DIM, (h + 1) * HEAD_DIM)
        q = q_ref[0, :, sl]
        s_c = jnp.where(mask_cur, _dot_t(q, kc_ref[0, :, sl]) * HEAD_SCALE, NEG_INF)
        s_p = jnp.where(mask_prev, _dot_t(q, kp_ref[0, :, sl]) * HEAD_SCALE, NEG_INF)
        m = jnp.maximum(jnp.max(s_c, -1, keepdims=True), jnp.max(s_p, -1, keepdims=True))
        e_c = jnp.exp(s_c - m)
        e_p = jnp.exp(s_p - m)
        l = jnp.sum(e_c, -1, keepdims=True) + jnp.sum(e_p, -1, keepdims=True)
        o = _dot(e_c.astype(BF16), vc_ref[0, :, sl]) + _dot(e_p.astype(BF16), vp_ref[0, :, sl])
        o_ref[0, :, sl] = o / l
        lse_ref[0, :, sl] = jnp.broadcast_to(m + jnp.log(l), (t, HEAD_DIM))


def _band_attn(q, k, v):
    s, l, w = q.shape
    t = DIL_KEYS
    cur = lambda a, i: (a, i, 0)
    prev = lambda a, i: (a, jnp.maximum(i - 1, 0), 0)
    blk = (1, t, w)
    return pl.pallas_call(
        _band_attn_kernel,
        grid=(s, l // t),
        in_specs=[pl.BlockSpec(blk, cur), pl.BlockSpec(blk, prev), pl.BlockSpec(blk, cur),
                  pl.BlockSpec(blk, prev), pl.BlockSpec(blk, cur)],
        out_specs=[pl.BlockSpec(blk, cur), pl.BlockSpec(blk, cur)],
        out_shape=[jax.ShapeDtypeStruct((s, l, w), F32), jax.ShapeDtypeStruct((s, l, w), F32)],
        compiler_params=_cparams(2),
        name="band_attn",
    )(q, k, k, v, v)


def _dil_sample_kernel(q_ref, kn_ref, vn_ref, *refs, bb):
    k_refs, v_refs, o_ref = refs[:N_DIL_GROUPS], refs[N_DIL_GROUPS:2 * N_DIL_GROUPS], refs[2 * N_DIL_GROUPS]
    for j in range(bb):
        outs, lses = [], []
        for g in range(N_DIL_GROUPS):
            q_row = q_ref[pl.ds(j, 1), g * DIL_KV_DIM:(g + 1) * DIL_KV_DIM]
            k_heads, v_heads = [], []
            for h in range(DIL_KV_HEADS):
                sl = slice(h * HEAD_DIM, (h + 1) * HEAD_DIM)
                k_new = jnp.broadcast_to(kn_ref[pl.ds(j, 1), sl], (8, HEAD_DIM))
                v_new = jnp.broadcast_to(vn_ref[pl.ds(j, 1), sl], (8, HEAD_DIM))
                k_heads.append(jnp.concatenate([k_refs[g][j, :, h, :], k_new], 0).astype(BF16))
                v_heads.append(jnp.concatenate([v_refs[g][j, :, h, :], v_new], 0).astype(BF16))
            o, lse = _single_query_attend(q_row, k_heads, v_heads, DIL_KEYS + 1, False)
            outs.append(o)
            lses.append(lse)
        mx = functools.reduce(jnp.maximum, lses)
        ws = [jnp.exp(x - mx) for x in lses]
        mix = sum(o * w for o, w in zip(outs, ws)) / sum(ws)
        o_ref[pl.ds(j, 1), :] = _collapse_heads(mix).astype(o_ref.dtype)


def _dil_sample(q, k_new, v_new, cache_k, cache_v):
    b, wbuf = cache_k.shape[:2]
    bb = 8
    views, specs = [], []
    for window, d in DIL_PATTERNS:
        assert wbuf % (d * DIL_KEYS) == 0 and wbuf >= window
        last = wbuf // d // DIL_KEYS - 1
        specs.append(pl.BlockSpec((bb, DIL_KEYS, None, DIL_KV_HEADS, HEAD_DIM),
                                  functools.partial(lambda i, last_: (i, last_, 0, 0, 0), last_=last)))
        views.append(lambda c, d_=d: c.reshape(b, wbuf // d_, d_, DIL_KV_HEADS, HEAD_DIM))
    row = lambda i: (i, 0)
    return pl.pallas_call(
        functools.partial(_dil_sample_kernel, bb=bb),
        grid=(b // bb,),
        in_specs=[pl.BlockSpec((bb, MIX_HEADS * HEAD_DIM), row), pl.BlockSpec((bb, DIL_KV_DIM), row),
                  pl.BlockSpec((bb, DIL_KV_DIM), row)] + specs + specs,
        out_specs=pl.BlockSpec((bb, DIL_KV_DIM), row),
        out_shape=jax.ShapeDtypeStruct((b, DIL_KV_DIM), BF16),
        compiler_params=_cparams(1),
        name="dil_sample",
    )(q, k_new, v_new, *[f(cache_k) for f in views], *[f(cache_v) for f in views])


SWA_SHIFT_BATCH = 2
SWA_SHIFT_BUFFERS = 3


def _swa_shift_kernel(c_hbm, new_hbm, o_hbm, buf, in_sem, out_sem, new_sem):
    b, w = c_hbm.shape[:2]
    nbuf, bb = buf.shape[:2]
    n = b // bb

    def copy_in(c, slot):
        return pltpu.make_async_copy(c_hbm.at[pl.ds(c * bb, bb)], buf.at[slot], in_sem.at[slot])

    def copy_out(c, slot):
        return pltpu.make_async_copy(buf.at[slot, :, pl.ds(1, w - 1)],
                                     o_hbm.at[pl.ds(c * bb, bb), pl.ds(0, w - 1)], out_sem.at[slot])

    new_copy = pltpu.make_async_copy(new_hbm, o_hbm.at[:, pl.ds(w - 1, 1)], new_sem)
    new_copy.start()
    for c in range(min(nbuf - 1, n)):
        copy_in(c, c).start()

    def body(c, carry):
        slot = c % nbuf
        copy_in(c, slot).wait()
        copy_out(c, slot).start()

        @pl.when(c >= 1)
        def _():
            copy_out(c - 1, (c - 1) % nbuf).wait()

        @pl.when(c + nbuf - 1 < n)
        def _():
            copy_in(c + nbuf - 1, (c + nbuf - 1) % nbuf).start()

        return carry

    lax.fori_loop(0, n, body, 0)
    copy_out(n - 1, (n - 1) % nbuf).wait()
    new_copy.wait()


def _swa_shift(cache, new):
    b = cache.shape[0]
    bb = SWA_SHIFT_BATCH
    assert b % bb == 0
    any_spec = pl.BlockSpec(memory_space=pl.ANY)
    return pl.pallas_call(
        _swa_shift_kernel,
        in_specs=[any_spec, any_spec],
        out_specs=any_spec,
        out_shape=jax.ShapeDtypeStruct(cache.shape, cache.dtype),
        scratch_shapes=[pltpu.VMEM((SWA_SHIFT_BUFFERS, bb) + cache.shape[1:], cache.dtype),
                        pltpu.SemaphoreType.DMA((SWA_SHIFT_BUFFERS,)), pltpu.SemaphoreType.DMA((SWA_SHIFT_BUFFERS,)),
                        pltpu.SemaphoreType.DMA(())],
        compiler_params=pltpu.CompilerParams(vmem_limit_bytes=VMEM_LIMIT),
        name="swa_shift",
    )(cache, new)


ROUTE_GROUP_COL = 0
ROUTE_EXPERT_COL = 8


def _layer_norm(s, g, b):
    mu = jnp.mean(s, -1, keepdims=True)
    d = s - mu
    var = jnp.mean(d * d, -1, keepdims=True)
    return d * lax.rsqrt(var + LN_EPS) * g + b


def _route(x, w, bias):
    logits = _dot(x.astype(BF16), w) + bias
    col = lax.broadcasted_iota(jnp.int32, logits.shape, 1).astype(F32)
    big = float(LANES)
    is_g = col < N_EXPERT_GROUPS
    gl = jnp.where(is_g, logits, NEG_INF)
    gmax = jnp.max(gl, -1, keepdims=True)
    g_sel = jnp.min(jnp.where(gl == gmax, col, big), -1, keepdims=True)
    g_w = 1.0 / jnp.sum(jnp.where(is_g, jnp.exp(gl - gmax), 0.0), -1, keepdims=True)
    lo = ROUTE_EXPERT_COL + EXPERTS_PER_GROUP * g_sel
    el = jnp.where(jnp.logical_and(col >= lo, col < lo + EXPERTS_PER_GROUP), logits, NEG_INF)
    m1 = jnp.max(el, -1, keepdims=True)
    i1 = jnp.min(jnp.where(el == m1, col, big), -1, keepdims=True)
    el2 = jnp.where(col == i1, NEG_INF, el)
    m2 = jnp.max(el2, -1, keepdims=True)
    i2 = jnp.min(jnp.where(el2 == m2, col, big), -1, keepdims=True)
    e2 = jnp.exp(m2 - m1)
    w1 = g_w / (1.0 + e2)
    w2 = g_w * e2 / (1.0 + e2)
    out = jnp.where(col == 0, i1 - ROUTE_EXPERT_COL, 0.0)
    out = jnp.where(col == 1, i2 - ROUTE_EXPERT_COL, out)
    out = jnp.where(col == 2, w1, out)
    return jnp.where(col == 3, w2, out)


def _attn_out_tail(delta, h_ref, g_ref, b_ref, wr_ref, rb_ref, h1_ref, rows_ref, route_ref):
    h1 = _layer_norm(DEEPNORM_ALPHA * h_ref[...] + delta, g_ref[...], b_ref[...])
    h1_ref[...] = h1
    _store_token_rows(rows_ref, h1)
    route_ref[...] = _route(h1, wr_ref[...], rb_ref[...])


def _attn_out_mla_kernel(mix_ref, mem_ref, w_ref, *tail):
    n_mix = mix_ref.shape[1]
    delta = _dot(mix_ref[...], w_ref[0:n_mix, :]) + _dot(mem_ref[...], w_ref[n_mix:, :])
    _attn_out_tail(delta, *tail)


def _attn_out_dil_kernel(o0_ref, o1_ref, o2_ref, l0_ref, l1_ref, l2_ref, mem_ref, w_ref, *tail):
    l0, l1, l2 = l0_ref[...], l1_ref[...], l2_ref[...]
    mx = jnp.maximum(jnp.maximum(l0, l1), l2)
    e0, e1, e2 = jnp.exp(l0 - mx), jnp.exp(l1 - mx), jnp.exp(l2 - mx)
    mix = (o0_ref[...] * e0 + o1_ref[...] * e1 + o2_ref[...] * e2) / (e0 + e1 + e2)
    n_mix = mix.shape[1]
    delta = _dot(mix.astype(BF16), w_ref[0:n_mix, :]) + _dot(mem_ref[...], w_ref[n_mix:, :])
    _attn_out_tail(delta, *tail)


def _attn_out(kernel_fn, parts, w, h, ln_g, ln_b, w_route, r_bias, name):
    m = h.shape[0]
    tm = min(_row_tile(m), 256)
    row = lambda i: (i, 0)
    fixed = lambda i: (0, 0)
    full = lambda a: pl.BlockSpec(a.shape, fixed)
    return pl.pallas_call(
        kernel_fn,
        grid=(m // tm,),
        in_specs=[pl.BlockSpec((tm, p.shape[1]), row) for p in parts]
        + [full(w), pl.BlockSpec((tm, D_MODEL), row), full(ln_g), full(ln_b), full(w_route), full(r_bias)],
        out_specs=[pl.BlockSpec((tm, D_MODEL), row), pl.BlockSpec((tm * ROW_CHUNKS, LANES), row),
                   pl.BlockSpec((tm, LANES), row)],
        out_shape=[jax.ShapeDtypeStruct((m, D_MODEL), F32), jax.ShapeDtypeStruct((m * ROW_CHUNKS, LANES), F32),
                   jax.ShapeDtypeStruct((m, LANES), F32)],
        compiler_params=_cparams(1),
        name=name,
    )(*parts, w, h, ln_g, ln_b, w_route, r_bias)


def _moe_num_tiles(n_tokens):
    worst_rows = 2 * n_tokens + N_EXPERTS * (MOE_TM - 1)
    return -(-worst_rows // MOE_TM)


def _moe_dispatch(route):
    n = route.shape[0]
    ids = route[:, 0:2].astype(jnp.int32).reshape(-1)
    gates = route[:, 2:4].reshape(-1)
    onehot = (ids[:, None] == jnp.arange(N_EXPERTS, dtype=jnp.int32)[None, :]).astype(jnp.int32)
    counts = jnp.sum(onehot, 0)
    rank = jnp.sum((jnp.cumsum(onehot, 0) - onehot) * onehot, 1)
    padded = (counts + MOE_TM - 1) // MOE_TM * MOE_TM
    ends = jnp.cumsum(padded)
    pos = (ends - padded)[ids] + rank
    n_tiles = _moe_num_tiles(n)
    rows = n_tiles * MOE_TM
    packed = jnp.stack([jnp.arange(2 * n, dtype=jnp.int32) // 2, lax.bitcast_convert_type(gates, jnp.int32)], axis=1)
    packed = jnp.zeros((rows, 2), jnp.int32).at[pos].set(packed)
    row_token, row_gate = packed[:, 0], lax.bitcast_convert_type(packed[:, 1], F32)
    tile_start = jnp.arange(n_tiles, dtype=jnp.int32) * MOE_TM
    tile_expert = jnp.minimum(jnp.searchsorted(ends, tile_start, side="right"), N_EXPERTS - 1).astype(jnp.int32)
    n_valid = (ends[-1] // MOE_TM).astype(jnp.int32).reshape(1)
    return tile_expert, n_valid, row_token, row_gate.reshape(rows, 1), pos.astype(jnp.int32)


def _row_copy(src_hbm, src_row, dst_buf, dst_row, sem):
    src = pl.ds(pl.multiple_of(src_row * ROW_CHUNKS, ROW_CHUNKS), ROW_CHUNKS)
    dst = pl.ds(pl.multiple_of(dst_row * ROW_CHUNKS, ROW_CHUNKS), ROW_CHUNKS)
    return pltpu.make_async_copy(src_hbm.at[src], dst_buf.at[dst], sem)


DMA_LOOP_UNROLL = 8


def _moe_ffn_kernel(te_ref, nv_ref, rt_ref, x_hbm, gate_ref, wg_ref, wu_ref, wd_ref, y_ref,
                    xbuf, sem, wg_b, wu_b, wd_b):
    i = pl.program_id(0)
    n_valid = nv_ref[0]
    tm = xbuf.shape[1] // ROW_CHUNKS

    def gather(tile, slot):
        def body(k, carry):
            _row_copy(x_hbm, rt_ref[tile * tm + k], xbuf.at[slot], k, sem.at[slot]).start()
            return carry
        lax.fori_loop(0, tm, body, 0, unroll=DMA_LOOP_UNROLL)

    def gather_wait(slot):
        def body(k, carry):
            _row_copy(x_hbm, 0, xbuf.at[slot], k, sem.at[slot]).wait()
            return carry
        lax.fori_loop(0, tm, body, 0, unroll=DMA_LOOP_UNROLL)

    @pl.when(jnp.logical_and(i == 0, n_valid > 0))
    def _():
        gather(0, 0)

    @pl.when(i + 1 < n_valid)
    def _():
        gather(i + 1, (i + 1) % 2)

    @pl.when(i < n_valid)
    def _():
        new_expert = jnp.logical_or(i == 0, te_ref[i] != te_ref[jnp.maximum(i - 1, 0)])

        @pl.when(new_expert)
        def _():
            wg_b[...] = wg_ref[0].astype(BF16)
            wu_b[...] = wu_ref[0].astype(BF16)
            wd_b[...] = wd_ref[0].astype(BF16)

        slot = i % 2
        gather_wait(slot)
        xb = _load_token_rows(xbuf.at[slot], tm).astype(BF16)
        g = _dot(xb, wg_b[...])
        u = _dot(xb, wu_b[...])
        hid = g / (1.0 + jnp.exp(-g)) * u * gate_ref[...]
        _store_token_rows(y_ref, _dot(hid.astype(BF16), wd_b[...]))

    @pl.when(i >= n_valid)
    def _():
        y_ref[...] = jnp.zeros_like(y_ref)


def _moe_ffn(x, w_g, w_u, w_d, layer, tile_expert, n_valid, row_token, row_gate):
    d = w_g.shape[2]
    hdim = w_g.shape[3]
    rows = row_token.shape[0]
    tm = MOE_TM
    n_tiles = rows // tm
    live = lambda i, te, nv, rt: jnp.minimum(i, jnp.maximum(nv[0] - 1, 0))
    grid_spec = pltpu.PrefetchScalarGridSpec(
        num_scalar_prefetch=3,
        grid=(n_tiles,),
        in_specs=[pl.BlockSpec(memory_space=pl.ANY),
                  pl.BlockSpec((tm, 1), lambda i, te, nv, rt: (live(i, te, nv, rt), 0)),
                  pl.BlockSpec((None, 1, d, hdim), lambda i, te, nv, rt: (layer, te[live(i, te, nv, rt)], 0, 0)),
                  pl.BlockSpec((None, 1, d, hdim), lambda i, te, nv, rt: (layer, te[live(i, te, nv, rt)], 0, 0)),
                  pl.BlockSpec((None, 1, hdim, d), lambda i, te, nv, rt: (layer, te[live(i, te, nv, rt)], 0, 0))],
        out_specs=pl.BlockSpec((tm * ROW_CHUNKS, LANES), lambda i, te, nv, rt: (i, 0)),
        scratch_shapes=[pltpu.VMEM((2, tm * ROW_CHUNKS, LANES), F32), pltpu.SemaphoreType.DMA((2,)),
                        pltpu.VMEM((d, hdim), BF16), pltpu.VMEM((d, hdim), BF16), pltpu.VMEM((hdim, d), BF16)],
    )
    return pl.pallas_call(
        _moe_ffn_kernel,
        grid_spec=grid_spec,
        out_shape=jax.ShapeDtypeStruct((rows * ROW_CHUNKS, LANES), F32),
        compiler_params=_cparams(1),
        name="moe_ffn",
    )(tile_expert, n_valid, row_token, x, row_gate, w_g, w_u, w_d)


def _moe_combine_kernel(pos_ref, h_ref, y_hbm, g_ref, b_ref, o_ref, ybuf, sem, *, tok0):
    i = pl.program_id(0)
    n_steps = pl.num_programs(0)
    tm = h_ref.shape[0]

    def gather(step, slot):
        def body(k, carry):
            base = 2 * (tok0 + step * tm + k)
            _row_copy(y_hbm, pos_ref[base], ybuf.at[slot, 0], k, sem.at[slot]).start()
            _row_copy(y_hbm, pos_ref[base + 1], ybuf.at[slot, 1], k, sem.at[slot]).start()
            return carry
        lax.fori_loop(0, tm, body, 0, unroll=DMA_LOOP_UNROLL)

    def gather_wait(slot):
        def body(k, carry):
            _row_copy(y_hbm, 0, ybuf.at[slot, 0], k, sem.at[slot]).wait()
            _row_copy(y_hbm, 0, ybuf.at[slot, 1], k, sem.at[slot]).wait()
            return carry
        lax.fori_loop(0, tm, body, 0, unroll=DMA_LOOP_UNROLL)

    @pl.when(i == 0)
    def _():
        gather(0, 0)

    @pl.when(i + 1 < n_steps)
    def _():
        gather(i + 1, (i + 1) % 2)

    slot = i % 2
    gather_wait(slot)
    s = DEEPNORM_ALPHA * h_ref[...] + (_load_token_rows(ybuf.at[slot, 0], tm) + _load_token_rows(ybuf.at[slot, 1], tm))
    o_ref[...] = _layer_norm(s, g_ref[...], b_ref[...])


def _moe_combine(pos, h1, y_sorted, ln_g, ln_b, tok0):
    n_rows, d = h1.shape
    tm = min(_row_tile(n_rows), 256)
    grid_spec = pltpu.PrefetchScalarGridSpec(
        num_scalar_prefetch=1,
        grid=(n_rows // tm,),
        in_specs=[pl.BlockSpec((tm, d), lambda i, p: (i, 0)),
                  pl.BlockSpec(memory_space=pl.ANY),
                  pl.BlockSpec((1, d), lambda i, p: (0, 0)), pl.BlockSpec((1, d), lambda i, p: (0, 0))],
        out_specs=pl.BlockSpec((tm, d), lambda i, p: (i, 0)),
        scratch_shapes=[pltpu.VMEM((2, 2, tm * ROW_CHUNKS, LANES), F32), pltpu.SemaphoreType.DMA((2,))],
    )
    return pl.pallas_call(
        functools.partial(_moe_combine_kernel, tok0=tok0),
        grid_spec=grid_spec,
        out_shape=jax.ShapeDtypeStruct((n_rows, d), F32),
        compiler_params=_cparams(1),
        name="moe_combine",
    )(pos, h1, y_sorted, ln_g, ln_b)


def _router_weights(w_rg, b_rg, w_re, b_re):
    d = w_rg.shape[0]
    w = jnp.zeros((d, LANES), F32)
    w = w.at[:, ROUTE_GROUP_COL:ROUTE_GROUP_COL + N_EXPERT_GROUPS].set(w_rg)
    w = w.at[:, ROUTE_EXPERT_COL:ROUTE_EXPERT_COL + N_EXPERTS].set(w_re)
    bias = jnp.zeros((1, LANES), F32)
    bias = bias.at[0, ROUTE_GROUP_COL:ROUTE_GROUP_COL + N_EXPERT_GROUPS].set(b_rg)
    bias = bias.at[0, ROUTE_EXPERT_COL:ROUTE_EXPERT_COL + N_EXPERTS].set(b_re)
    return w.astype(BF16), bias


def _moe_block(prompt, sample, w_g, w_u, w_d, layer, ln_g, ln_b):
    (h1_p, rows_p, route_p), (h1_s, rows_s, route_s) = prompt, sample
    rows = jnp.concatenate([rows_p, rows_s], axis=0)
    route = jnp.concatenate([route_p, route_s], axis=0)
    tile_expert, n_valid, row_token, row_gate, pos = _moe_dispatch(route)
    y_sorted = _moe_ffn(rows, w_g, w_u, w_d, layer, tile_expert, n_valid, row_token, row_gate)
    out_p = _moe_combine(pos, h1_p, y_sorted, ln_g, ln_b, 0)
    out_s = _moe_combine(pos, h1_s, y_sorted, ln_g, ln_b, h1_p.shape[0])
    return out_p, out_s


def kernel(x_prompt, x_sample, cache_mla_latent, cache_mla_krope, cache_swa_k, cache_swa_v, cache_mem_k, cache_mem_v, page_table, mem_prompt, w_in_a, q_norm_a, kv_norm_a, w_q_up_a, w_kv_up_a, w_out_a, w_in_b, w_kv_shared, w_out_b, w_mem_kv, ln1_g, ln1_b, ln2_g, ln2_b, w_router_group, b_router_group, w_router_expert, b_router_expert, w_exp_gate, w_exp_up, w_exp_down):
    bp, tp, d = x_prompt.shape
    bs, ts, _ = x_sample.shape
    assert ts == 1 and w_in_a.shape[0] == 1 and w_in_b.shape[0] == 1
    n_p, n_s = bp * tp, bs * ts
    n_mem = mem_prompt.shape[1]
    wbuf = cache_swa_k.shape[1]
    past_len = page_table.shape[1] * PAGE_SIZE
    assert wbuf == DIL_PATTERNS[-1][0]

    pos_p = jnp.arange(tp, dtype=jnp.int32)
    pos_s = jnp.full((n_s,), past_len, jnp.int32)
    tm_p = _row_tile(n_p)
    assert tp % tm_p == 0
    tab_blocks_p = tp // tm_p
    tabs_r_p = _rope_tables(pos_p, ROPE_DIM, LANES)
    tabs_r_s = _rope_tables(pos_s, ROPE_DIM, LANES)
    tabs_h_p = _rope_tables(pos_p, HEAD_DIM, LANES)
    tabs_h_s = _rope_tables(pos_s, HEAD_DIM, LANES)

    hp = x_prompt.reshape(n_p, d)
    hs = x_sample.reshape(n_s, d)
    ln = lambda a, l: a[l].reshape(1, d)

    mem2d = mem_prompt.reshape(bp * n_mem, d)
    mkv = [_matmul(mem2d, w_mem_kv[l].astype(BF16)).reshape(bp, n_mem, 2 * MEM_DIM) for l in range(DEPTH)]

    w_in = w_in_a[0]
    o_kr = Q_LORA + KV_LORA
    w_in = jnp.concatenate([w_in[:, :o_kr + ROPE_DIM], jnp.zeros((d, LANES - ROPE_DIM), F32),
                            w_in[:, o_kr + ROPE_DIM:]], axis=1).astype(BF16)
    w_q = jnp.concatenate([w_q_up_a[0], jnp.zeros((Q_LORA, MIX_HEADS, MLA_QK - NOPE_DIM - ROPE_DIM), F32)], axis=-1)
    w_q = w_q.reshape(Q_LORA, MIX_HEADS * MLA_QK).astype(BF16)
    w_kv = w_kv_up_a[0].reshape(KV_LORA, MIX_HEADS * (NOPE_DIM + V_DIM)).astype(BF16)
    w_uk_t = jnp.transpose(w_kv_up_a[0][..., :NOPE_DIM], (1, 2, 0)).astype(BF16)
    w_uv = jnp.transpose(w_kv_up_a[0][..., NOPE_DIM:], (1, 0, 2)).astype(BF16)
    q_g, kv_g = q_norm_a[0].reshape(1, Q_LORA), kv_norm_a[0].reshape(1, KV_LORA)

    cq_p, lat_p, kr_p, qm_p = _mla_in_proj(hp, w_in, q_g, kv_g, tabs_r_p, tab_blocks_p)
    cq_s, lat_s, kr_s, qm_s = _mla_in_proj(hs, w_in, q_g, kv_g, tabs_r_s, 1)
    q_p = _q_up(cq_p, w_q, tabs_r_p, tab_blocks_p)
    q_s = _q_up(cq_s, w_q, tabs_r_s, 1)

    k_full, v_full = _kv_up(lat_p, kr_p, w_kv)
    mix_p = _mla_flash(q_p.reshape(bp, tp, -1), k_full.reshape(bp, tp, -1), v_full.reshape(bp, tp, -1)).reshape(n_p, -1)

    hpad = 16
    q_s3 = q_s.reshape(n_s, MIX_HEADS, MLA_QK)
    qlat = _head_matmul(jnp.transpose(q_s3[:, :, :NOPE_DIM], (1, 0, 2)), w_uk_t, BF16)
    qlat = jnp.pad(jnp.transpose(qlat, (1, 0, 2)), ((0, 0), (0, hpad - MIX_HEADS), (0, 0)))
    qr = jnp.pad(q_s3[:, :, NOPE_DIM:], ((0, 0), (0, hpad - MIX_HEADS), (0, 0)))
    o_lat = _mla_decode(qlat, qr, lat_s.reshape(n_s, 1, KV_LORA), kr_s.reshape(n_s, 1, LANES),
                        cache_mla_latent, cache_mla_krope, page_table, 0)
    mix_s = _head_matmul(jnp.transpose(o_lat[:, :MIX_HEADS], (1, 0, 2)), w_uv, BF16)
    mix_s = jnp.transpose(mix_s, (1, 0, 2)).reshape(n_s, MIX_HEADS * V_DIM)

    mem_p = _mem_attn_prompt(qm_p.reshape(bp, tp, MEM_DIM), mkv[0]).reshape(n_p, MEM_DIM)
    mem_s = _mem_attn_sample(qm_s, cache_mem_k, cache_mem_v, 0)

    router = [_router_weights(w_router_group[l], b_router_group[l], w_router_expert[l], b_router_expert[l])
              for l in range(DEPTH)]
    w_out = w_out_a[0].astype(BF16)
    out_p = _attn_out(_attn_out_mla_kernel, [mix_p, mem_p], w_out, hp, ln(ln1_g, 0), ln(ln1_b, 0),
                      *router[0], name="attn_out_mla")
    out_s = _attn_out(_attn_out_mla_kernel, [mix_s, mem_s], w_out, hs, ln(ln1_g, 0), ln(ln1_b, 0),
                      *router[0], name="attn_out_mla")
    hp, hs = _moe_block(out_p, out_s, w_exp_gate, w_exp_up, w_exp_down, 0, ln(ln2_g, 0), ln(ln2_b, 0))

    n_q = MIX_HEADS * HEAD_DIM
    w_b = w_in_b[0].astype(BF16)
    w_kvs = w_kv_shared.astype(BF16)
    q_segs = tuple((g * DIL_KV_DIM, DIL_KV_DIM, True, 0, g * DIL_KV_DIM) for g in range(N_DIL_GROUPS))
    q_segs += ((n_q, MEM_DIM, False, 1, 0),)
    kv_segs = ((0, DIL_KV_DIM, True, 0, 0), (DIL_KV_DIM, DIL_KV_DIM, False, 1, 0))
    qd_p, qm_p = _proj_rope(hp, w_b, tabs_h_p, tab_blocks_p, q_segs, [(n_q, BF16), (MEM_DIM, BF16)])
    qd_s, qm_s = _proj_rope(hs, w_b, tabs_h_s, 1, q_segs, [(n_q, F32), (MEM_DIM, BF16)])
    k_p, v_p = _proj_rope(hp, w_kvs, tabs_h_p, tab_blocks_p, kv_segs, [(DIL_KV_DIM, F32), (DIL_KV_DIM, F32)])
    k_s, v_s = _proj_rope(hs, w_kvs, tabs_h_s, 1, kv_segs, [(DIL_KV_DIM, F32), (DIL_KV_DIM, F32)])

    k_pb, v_pb = k_p.astype(BF16), v_p.astype(BF16)
    outs, lses = [], []
    for g, (_, dil) in enumerate(DIL_PATTERNS):
        def split(a):
            a = a.reshape(bp, tp // dil, dil, DIL_KV_DIM)
            return jnp.transpose(a, (0, 2, 1, 3)).reshape(bp * dil, tp // dil, DIL_KV_DIM)

        def merge(a):
            a = a.reshape(bp, dil, tp // dil, DIL_KV_DIM)
            return jnp.transpose(a, (0, 2, 1, 3)).reshape(n_p, DIL_KV_DIM)

        o_g, lse_g = _band_attn(split(qd_p[:, g * DIL_KV_DIM:(g + 1) * DIL_KV_DIM]), split(k_pb), split(v_pb))
        outs.append(merge(o_g))
        lses.append(merge(lse_g))

    mix_s = _dil_sample(qd_s, k_s, v_s, cache_swa_k, cache_swa_v)
    mem_p = _mem_attn_prompt(qm_p.reshape(bp, tp, MEM_DIM), mkv[1]).reshape(n_p, MEM_DIM)
    mem_s = _mem_attn_sample(qm_s, cache_mem_k, cache_mem_v, 1)

    w_out = w_out_b[0].astype(BF16)
    out_p = _attn_out(_attn_out_dil_kernel, outs + lses + [mem_p], w_out, hp, ln(ln1_g, 1), ln(ln1_b, 1),
                      *router[1], name="attn_out_dil")
    out_s = _attn_out(_attn_out_mla_kernel, [mix_s, mem_s], w_out, hs, ln(ln1_g, 1), ln(ln1_b, 1),
                      *router[1], name="attn_out_mla")
    hp, hs = _moe_block(out_p, out_s, w_exp_gate, w_exp_up, w_exp_down, 1, ln(ln2_g, 1), ln(ln2_b, 1))

    kv4 = lambda a, n: a.reshape(n, -1, DIL_KV_HEADS, HEAD_DIM)
    new_k_s = _swa_shift(cache_swa_k, kv4(k_s, bs))
    new_v_s = _swa_shift(cache_swa_v, kv4(v_s, bs))
    keep_p = min(wbuf, tp)
    k_p4, v_p4 = kv4(k_p, bp), kv4(v_p, bp)
    mem_k = jnp.stack([m[:, :, :MEM_DIM].reshape(bp, n_mem, MEM_HEADS, HEAD_DIM) for m in mkv], axis=2)
    mem_v = jnp.stack([m[:, :, MEM_DIM:].reshape(bp, n_mem, MEM_HEADS, HEAD_DIM) for m in mkv], axis=2)
    return (hp.reshape(bp, tp, d), hs.reshape(bs, ts, d),
            lat_p.reshape(bp, tp, 1, KV_LORA), kr_p[:, :ROPE_DIM].reshape(bp, tp, 1, ROPE_DIM),
            lat_s.reshape(bs, ts, 1, KV_LORA), kr_s[:, :ROPE_DIM].reshape(bs, ts, 1, ROPE_DIM),
            k_p4[:, tp - keep_p:], v_p4[:, tp - keep_p:], new_k_s, new_v_s, mem_k, mem_v)
```

```python
import functools

import jax
import jax.numpy as jnp
import numpy as np
from jax import lax
from jax.experimental import pallas as pl
from jax.experimental.pallas import tpu as pltpu

BF16 = jnp.bfloat16
F32 = jnp.float32

D_MODEL = 2048
DEPTH = 2
HEAD_DIM = 128
MIX_HEADS = 12
MEM_HEADS = 4
MEM_DIM = MEM_HEADS * HEAD_DIM
Q_LORA = 512
KV_LORA = 256
NOPE_DIM = 128
ROPE_DIM = 64
V_DIM = 128
PAGE_SIZE = 128
DIL_PATTERNS = ((128, 1), (512, 4), (2048, 16))
N_DIL_GROUPS = len(DIL_PATTERNS)
DIL_KV_HEADS = MIX_HEADS // N_DIL_GROUPS
DIL_KV_DIM = DIL_KV_HEADS * HEAD_DIM
DIL_KEYS = 128
N_EXPERT_GROUPS = 4
EXPERTS_PER_GROUP = 8
N_EXPERTS = N_EXPERT_GROUPS * EXPERTS_PER_GROUP
EXPERT_HIDDEN = 512
ROPE_THETA = 10000.0
LN_EPS = 1e-5
RMS_EPS = 1e-6
DEEPNORM_ALPHA = (2.0 * DEPTH) ** 0.25
MLA_SCALE = (NOPE_DIM + ROPE_DIM) ** -0.5
HEAD_SCALE = HEAD_DIM ** -0.5
MLA_EXP2_SCALE = MLA_SCALE * float(np.log2(np.e))

LANES = 128
MLA_QK = 2 * LANES
VMEM_LIMIT = 56 * 1024 * 1024
MOE_TM = 256
MLA_PAGES_PER_STEP = 16
NEG_INF = float("-inf")
ROW_CHUNKS = D_MODEL // LANES


def _cparams(n_grid):
    return pltpu.CompilerParams(dimension_semantics=("arbitrary",) * n_grid, vmem_limit_bytes=VMEM_LIMIT)


def _row_tile(m):
    for t in (512, 256, 128, 64, 32, 16, 8):
        if m % t == 0:
            return t
    raise ValueError(f"row count {m} not a multiple of 8")


def _load_token_rows(ref, n):
    return jnp.concatenate([ref[pl.ds(c, n, stride=ROW_CHUNKS), :] for c in range(ROW_CHUNKS)], axis=-1)


def _store_token_rows(ref, val):
    n = val.shape[0]
    for c in range(ROW_CHUNKS):
        ref[pl.ds(c, n, stride=ROW_CHUNKS), :] = val[:, c * LANES:(c + 1) * LANES]


def _dot(a, b):
    return jnp.dot(a, b, preferred_element_type=F32)


def _dot_t(a, b):
    return lax.dot_general(a, b, (((1,), (1,)), ((), ())), preferred_element_type=F32)


def _rope_tables(pos, dim, width):
    half = dim // 2
    inv_freq = ROPE_THETA ** (-jnp.arange(0, dim, 2, dtype=F32) / dim)
    ang = pos.astype(F32)[:, None] * inv_freq[None, :]
    cos, sin = jnp.cos(ang), jnp.sin(ang)
    z = jnp.zeros((pos.shape[0], width - dim), F32)
    zh = jnp.zeros_like(cos)
    c = jnp.concatenate([cos, cos, z], axis=-1)
    s_lo = jnp.concatenate([-sin, zh, z], axis=-1)
    s_hi = jnp.concatenate([zh, sin, z], axis=-1)
    return c, s_lo, s_hi


def _rope_apply(x, c, s_lo, s_hi, half):
    width = x.shape[-1]
    return x * c + pltpu.roll(x, width - half, 1) * s_lo + pltpu.roll(x, half, 1) * s_hi


def _mm_kernel(x_ref, w_ref, o_ref):
    o_ref[...] = _dot(x_ref[...].astype(BF16), w_ref[...]).astype(o_ref.dtype)


def _matmul(x, w, out_dtype=F32):
    m, k = x.shape
    n = w.shape[1]
    tm = _row_tile(m)
    return pl.pallas_call(
        _mm_kernel,
        grid=(m // tm,),
        in_specs=[pl.BlockSpec((tm, k), lambda i: (i, 0)), pl.BlockSpec((k, n), lambda i: (0, 0))],
        out_specs=pl.BlockSpec((tm, n), lambda i: (i, 0)),
        out_shape=jax.ShapeDtypeStruct((m, n), out_dtype),
        compiler_params=_cparams(1),
        name="matmul",
    )(x, w)


def _bmm_kernel(x_ref, w_ref, o_ref):
    o_ref[0] = _dot(x_ref[0].astype(BF16), w_ref[0]).astype(o_ref.dtype)


def _head_matmul(x, w, out_dtype):
    h, m, k = x.shape
    n = w.shape[2]
    return pl.pallas_call(
        _bmm_kernel,
        grid=(h,),
        in_specs=[pl.BlockSpec((1, m, k), lambda i: (i, 0, 0)), pl.BlockSpec((1, k, n), lambda i: (i, 0, 0))],
        out_specs=pl.BlockSpec((1, m, n), lambda i: (i, 0, 0)),
        out_shape=jax.ShapeDtypeStruct((h, m, n), out_dtype),
        compiler_params=_cparams(1),
        name="head_matmul",
    )(x, w)


def _mla_in_kernel(x_ref, w_ref, qg_ref, kvg_ref, c_ref, slo_ref, shi_ref, cq_ref, lat_ref, kr_ref, qm_ref):
    xb = x_ref[...].astype(BF16)
    c_q = _dot(xb, w_ref[:, 0:Q_LORA])
    cq_ref[...] = (c_q * lax.rsqrt(jnp.mean(c_q * c_q, -1, keepdims=True) + RMS_EPS) * qg_ref[...]).astype(cq_ref.dtype)
    o = Q_LORA
    c_kv = _dot(xb, w_ref[:, o:o + KV_LORA])
    lat_ref[...] = c_kv * lax.rsqrt(jnp.mean(c_kv * c_kv, -1, keepdims=True) + RMS_EPS) * kvg_ref[...]
    o += KV_LORA
    k_rope = _dot(xb, w_ref[:, o:o + LANES])
    kr_ref[...] = _rope_apply(k_rope, c_ref[...], slo_ref[...], shi_ref[...], ROPE_DIM // 2)
    o += LANES
    qm_ref[...] = _dot(xb, w_ref[:, o:o + MEM_DIM]).astype(qm_ref.dtype)


def _mla_in_proj(x, w, q_g, kv_g, tabs, n_tab_blocks):
    m, d = x.shape
    tm = _row_tile(m)
    n = w.shape[1]
    row = lambda i: (i, 0)
    fixed = lambda i: (0, 0)
    tab = lambda i: (i % n_tab_blocks, 0)
    return pl.pallas_call(
        _mla_in_kernel,
        grid=(m // tm,),
        in_specs=[pl.BlockSpec((tm, d), row), pl.BlockSpec((d, n), fixed),
                  pl.BlockSpec((1, Q_LORA), fixed), pl.BlockSpec((1, KV_LORA), fixed),
                  pl.BlockSpec((tm, LANES), tab), pl.BlockSpec((tm, LANES), tab), pl.BlockSpec((tm, LANES), tab)],
        out_specs=[pl.BlockSpec((tm, Q_LORA), row), pl.BlockSpec((tm, KV_LORA), row),
                   pl.BlockSpec((tm, LANES), row), pl.BlockSpec((tm, MEM_DIM), row)],
        out_shape=[jax.ShapeDtypeStruct((m, Q_LORA), BF16), jax.ShapeDtypeStruct((m, KV_LORA), F32),
                   jax.ShapeDtypeStruct((m, LANES), F32), jax.ShapeDtypeStruct((m, MEM_DIM), BF16)],
        compiler_params=_cparams(1),
        name="mla_in_proj",
    )(x, w, q_g, kv_g, *tabs)


def _q_up_kernel(cq_ref, w_ref, c_ref, slo_ref, shi_ref, q_ref):
    cq = cq_ref[...]
    c, slo, shi = c_ref[...], slo_ref[...], shi_ref[...]
    for h in range(MIX_HEADS):
        r = _dot(cq, w_ref[:, h * MLA_QK:(h + 1) * MLA_QK])
        q_ref[:, h * MLA_QK:h * MLA_QK + LANES] = r[:, :LANES].astype(q_ref.dtype)
        roped = _rope_apply(r[:, LANES:], c, slo, shi, ROPE_DIM // 2)
        q_ref[:, h * MLA_QK + LANES:(h + 1) * MLA_QK] = roped.astype(q_ref.dtype)


def _q_up(cq, w, tabs, n_tab_blocks):
    m, k = cq.shape
    tm = _row_tile(m)
    n = w.shape[1]
    row = lambda i: (i, 0)
    fixed = lambda i: (0, 0)
    tab = lambda i: (i % n_tab_blocks, 0)
    return pl.pallas_call(
        _q_up_kernel,
        grid=(m // tm,),
        in_specs=[pl.BlockSpec((tm, k), row), pl.BlockSpec((k, n), fixed),
                  pl.BlockSpec((tm, LANES), tab), pl.BlockSpec((tm, LANES), tab), pl.BlockSpec((tm, LANES), tab)],
        out_specs=pl.BlockSpec((tm, n), row),
        out_shape=jax.ShapeDtypeStruct((m, n), BF16),
        compiler_params=_cparams(1),
        name="mla_q_up",
    )(cq, w, *tabs)


def _kv_up_kernel(lat_ref, kr_ref, w_ref, k_ref, v_ref):
    lat = lat_ref[...].astype(BF16)
    kr = kr_ref[...].astype(BF16)
    for h in range(MIX_HEADS):
        r = _dot(lat, w_ref[:, h * 2 * LANES:(h + 1) * 2 * LANES])
        k_ref[:, h * MLA_QK:h * MLA_QK + LANES] = r[:, :LANES].astype(BF16)
        k_ref[:, h * MLA_QK + LANES:(h + 1) * MLA_QK] = kr
        v_ref[:, h * V_DIM:(h + 1) * V_DIM] = r[:, LANES:].astype(BF16)


def _kv_up(lat, kr, w):
    m = lat.shape[0]
    tm = _row_tile(m)
    row = lambda i: (i, 0)
    return pl.pallas_call(
        _kv_up_kernel,
        grid=(m // tm,),
        in_specs=[pl.BlockSpec((tm, KV_LORA), row), pl.BlockSpec((tm, LANES), row),
                  pl.BlockSpec(w.shape, lambda i: (0, 0))],
        out_specs=[pl.BlockSpec((tm, MIX_HEADS * MLA_QK), row), pl.BlockSpec((tm, MIX_HEADS * V_DIM), row)],
        out_shape=[jax.ShapeDtypeStruct((m, MIX_HEADS * MLA_QK), BF16),
                   jax.ShapeDtypeStruct((m, MIX_HEADS * V_DIM), BF16)],
        compiler_params=_cparams(1),
        name="mla_kv_up",
    )(lat, kr, w)


SWA_SHIFT_BATCH = 2
SWA_SHIFT_BUFFERS = 3


def _swa_shift_step(t, srcs, dsts, buf, in_sem, out_sem):
    nbuf, bb = buf.shape[:2]
    b, w = srcs[0].shape[:2]
    per = b // bb
    n = per * len(srcs)

    def start_in(c, slot):
        for a, src in enumerate(srcs):
            @pl.when(jnp.logical_and(c >= a * per, c < (a + 1) * per))
            def _():
                pltpu.make_async_copy(src.at[pl.ds((c - a * per) * bb, bb)], buf.at[slot], in_sem.at[slot]).start()

    def out_copies(dst, row0, slot):
        return (pltpu.make_async_copy(buf.at[slot, :, pl.ds(1, w - 1)],
                                      dst.at[pl.ds(row0, bb), pl.ds(0, w - 1)], out_sem.at[slot]),
                pltpu.make_async_copy(buf.at[slot, :, pl.ds(w - 1, 1)],
                                      dst.at[pl.ds(row0, bb), pl.ds(w - 1, 1)], out_sem.at[slot]))

    def start_out(c, slot):
        for a, dst in enumerate(dsts):
            @pl.when(jnp.logical_and(c >= a * per, c < (a + 1) * per))
            def _():
                for cp in out_copies(dst, (c - a * per) * bb, slot):
                    cp.start()

    @pl.when(t == 0)
    def _():
        for c in range(min(nbuf - 1, n)):
            start_in(c, c)

    @pl.when(t < n)
    def _():
        slot = t % nbuf
        pltpu.make_async_copy(srcs[0].at[pl.ds(0, bb)], buf.at[slot], in_sem.at[slot]).wait()
        start_out(t, slot)

    @pl.when(jnp.logical_and(t >= 1, t <= n))
    def _():
        for cp in out_copies(dsts[0], 0, (t - 1) % nbuf):
            cp.wait()

    @pl.when(t + nbuf - 1 < n)
    def _():
        start_in(t + nbuf - 1, (t + nbuf - 1) % nbuf)


def _swa_set_newest_kernel(shifted_hbm, new_hbm, o_hbm, sem):
    del shifted_hbm
    w = o_hbm.shape[1]
    cp = pltpu.make_async_copy(new_hbm, o_hbm.at[:, pl.ds(w - 1, 1)], sem)
    cp.start()
    cp.wait()


def _swa_set_newest(shifted, new):
    any_spec = pl.BlockSpec(memory_space=pl.ANY)
    return pl.pallas_call(
        _swa_set_newest_kernel,
        in_specs=[any_spec, any_spec],
        out_specs=any_spec,
        out_shape=jax.ShapeDtypeStruct(shifted.shape, shifted.dtype),
        scratch_shapes=[pltpu.SemaphoreType.DMA(())],
        input_output_aliases={0: 0},
        name="swa_set_newest",
    )(shifted, new)


def _mla_flash_kernel(q_ref, k_ref, v_ref, ck_hbm, cv_hbm, o_ref, ok_hbm, ov_hbm, buf, in_sem, out_sem, *, tq):
    i = pl.program_id(2)
    step = (pl.program_id(0) * pl.num_programs(1) + pl.program_id(1)) * pl.num_programs(2) + i
    _swa_shift_step(step, (ck_hbm, cv_hbm), (ok_hbm, ov_hbm), buf, in_sem, out_sem)
    q = q_ref[0]

    def step(j, carry, masked):
        m, l, acc = carry
        start = pl.multiple_of(j * tq, tq)
        s = _dot_t(q, k_ref[0, pl.ds(start, tq), :])
        if masked:
            rows = lax.broadcasted_iota(jnp.int32, s.shape, 0)
            cols = lax.broadcasted_iota(jnp.int32, s.shape, 1)
            s = jnp.where(cols <= rows, s, NEG_INF)
        m_new = jnp.maximum(m, jnp.max(s, -1, keepdims=True))
        corr = jnp.exp2((m - m_new) * MLA_EXP2_SCALE)
        p = jnp.exp2((s - m_new) * MLA_EXP2_SCALE)
        l = l * corr + jnp.sum(p, -1, keepdims=True)
        acc = acc * corr + _dot(p.astype(BF16), v_ref[0, pl.ds(start, tq), :])
        return m_new, l, acc

    init = (jnp.full((tq, 1), NEG_INF, F32), jnp.zeros((tq, 1), F32), jnp.zeros((tq, V_DIM), F32))
    carry = lax.fori_loop(0, i, lambda j, c: step(j, c, False), init)
    _, l, acc = step(i, carry, True)
    o_ref[0] = (acc / l).astype(o_ref.dtype)


def _mla_flash(q, k, v, cache_k, cache_v):
    b, t, _ = q.shape
    tq = _row_tile(t)
    grid = (b, MIX_HEADS, t // tq)
    bs = cache_k.shape[0]
    bb = SWA_SHIFT_BATCH
    assert bs % bb == 0 and 2 * (bs // bb) < grid[0] * grid[1] * grid[2]
    any_spec = pl.BlockSpec(memory_space=pl.ANY)
    return pl.pallas_call(
        functools.partial(_mla_flash_kernel, tq=tq),
        grid=grid,
        in_specs=[pl.BlockSpec((1, tq, MLA_QK), lambda b_, h, i: (b_, i, h)),
                  pl.BlockSpec((1, t, MLA_QK), lambda b_, h, i: (b_, 0, h)),
                  pl.BlockSpec((1, t, V_DIM), lambda b_, h, i: (b_, 0, h)), any_spec, any_spec],
        out_specs=[pl.BlockSpec((1, tq, V_DIM), lambda b_, h, i: (b_, i, h)), any_spec, any_spec],
        out_shape=[jax.ShapeDtypeStruct((b, t, MIX_HEADS * V_DIM), BF16),
                   jax.ShapeDtypeStruct(cache_k.shape, cache_k.dtype), jax.ShapeDtypeStruct(cache_v.shape, cache_v.dtype)],
        scratch_shapes=[pltpu.VMEM((SWA_SHIFT_BUFFERS, bb) + cache_k.shape[1:], cache_k.dtype),
                        pltpu.SemaphoreType.DMA((SWA_SHIFT_BUFFERS,)), pltpu.SemaphoreType.DMA((SWA_SHIFT_BUFFERS,))],
        compiler_params=_cparams(3),
        name="mla_flash",
    )(q, k, v, cache_k, cache_v)


def _mem_attn_kernel(q_ref, k_ref, v_ref, o_ref):
    for h in range(MEM_HEADS):
        sl = slice(h * HEAD_DIM, (h + 1) * HEAD_DIM)
        s = _dot_t(q_ref[0, :, sl], k_ref[0, :, sl].astype(BF16)) * HEAD_SCALE
        p = jnp.exp(s - jnp.max(s, -1, keepdims=True))
        p = p / jnp.sum(p, -1, keepdims=True)
        o_ref[0, :, sl] = _dot(p.astype(BF16), v_ref[0, :, sl].astype(BF16)).astype(o_ref.dtype)


def _mem_attn_prompt(q, mkv):
    b, t, _ = q.shape
    n_mem = mkv.shape[1]
    tq = _row_tile(t)
    return pl.pallas_call(
        _mem_attn_kernel,
        grid=(b, t // tq),
        in_specs=[pl.BlockSpec((1, tq, MEM_DIM), lambda b_, i: (b_, i, 0)),
                  pl.BlockSpec((1, n_mem, MEM_DIM), lambda b_, i: (b_, 0, 0)),
                  pl.BlockSpec((1, n_mem, MEM_DIM), lambda b_, i: (b_, 0, 1))],
        out_specs=pl.BlockSpec((1, tq, MEM_DIM), lambda b_, i: (b_, i, 0)),
        out_shape=jax.ShapeDtypeStruct((b, t, MEM_DIM), BF16),
        compiler_params=_cparams(2),
        name="mem_attn_prompt",
    )(q, mkv, mkv)


def _head_rows(width, rows=8):
    r = lax.broadcasted_iota(jnp.int32, (rows, width), 0)
    c = lax.broadcasted_iota(jnp.int32, (rows, width), 1)
    return (c // HEAD_DIM) == r


def _single_query_attend(q_row, k_heads, v_heads, valid_rows, normalise_first):
    n = k_heads[0].shape[0]
    row_id = lax.broadcasted_iota(jnp.int32, (8, HEAD_DIM), 0)
    s = None
    for h, k_h in enumerate(k_heads):
        q_h = jnp.broadcast_to(q_row[:, h * HEAD_DIM:(h + 1) * HEAD_DIM], (8, HEAD_DIM))
        s_h = _dot_t(k_h, jnp.where(row_id == h, q_h, 0.0).astype(BF16))
        s = s_h if s is None else s + s_h
    s = s * HEAD_SCALE
    if valid_rows < n:
        s = jnp.where(lax.broadcasted_iota(jnp.int32, s.shape, 0) < valid_rows, s, NEG_INF)
    m = jnp.max(s, 0, keepdims=True)
    e = jnp.exp(s - m)
    l = jnp.sum(e, 0, keepdims=True)
    eb = (e / l if normalise_first else e).astype(BF16)
    o = jnp.concatenate(
        [lax.dot_general(eb, v_h, (((0,), (0,)), ((), ())), preferred_element_type=F32) for v_h in v_heads], axis=-1)
    eye = lax.broadcasted_iota(jnp.int32, (8, 8), 0) == lax.broadcasted_iota(jnp.int32, (8, 8), 1)
    lse = jnp.sum(jnp.where(eye, jnp.broadcast_to(m + jnp.log(l), (8, 8)), 0.0), -1, keepdims=True)
    if normalise_first:
        return o, lse
    l_col = jnp.sum(jnp.where(eye, jnp.broadcast_to(l, (8, 8)), 0.0), -1, keepdims=True)
    return o / l_col, lse


def _collapse_heads(o):
    return jnp.sum(jnp.where(_head_rows(o.shape[1]), o, 0.0), 0, keepdims=True)


def _mem_attn_sample_kernel(q_ref, k_ref, v_ref, o_ref, *, bb, layer, n_layers):
    stride = n_layers * MEM_HEADS
    n_mem = k_ref.shape[1] // stride
    for j in range(bb):
        k = [k_ref[j, pl.ds(layer * MEM_HEADS + h, n_mem, stride=stride), :].astype(BF16) for h in range(MEM_HEADS)]
        v = [v_ref[j, pl.ds(layer * MEM_HEADS + h, n_mem, stride=stride), :].astype(BF16) for h in range(MEM_HEADS)]
        o, _ = _single_query_attend(q_ref[pl.ds(j, 1), :].astype(F32), k, v, n_mem, True)
        o_ref[pl.ds(j, 1), :] = _collapse_heads(o).astype(o_ref.dtype)


def _mem_attn_sample(q, cache_k, cache_v, layer):
    b = q.shape[0]
    n_mem, n_layers = cache_k.shape[1:3]
    bb = 8
    rows = n_mem * n_layers * MEM_HEADS
    cache_k, cache_v = cache_k.reshape(b, rows, HEAD_DIM), cache_v.reshape(b, rows, HEAD_DIM)
    cspec = pl.BlockSpec((bb, rows, HEAD_DIM), lambda i: (i, 0, 0))
    return pl.pallas_call(
        functools.partial(_mem_attn_sample_kernel, bb=bb, layer=layer, n_layers=n_layers),
        grid=(b // bb,),
        in_specs=[pl.BlockSpec((bb, MEM_DIM), lambda i: (i, 0)), cspec, cspec],
        out_specs=pl.BlockSpec((bb, MEM_DIM), lambda i: (i, 0)),
        out_shape=jax.ShapeDtypeStruct((b, MEM_DIM), BF16),
        compiler_params=_cparams(1),
        name="mem_attn_sample",
    )(q, cache_k, cache_v)


def _mla_decode_kernel(pt_ref, qlat_ref, qr_ref, latn_ref, krn_ref, lat_hbm, kr_hbm, o_ref,
                       lat_buf, kr_buf, sem, m_sc, l_sc, acc_sc, *, layer, n_layers):
    pages, page_rows = lat_buf.shape[1], lat_buf.shape[2]
    n_chunks = pl.num_programs(1)
    c = pl.program_id(1)
    step = pl.program_id(0) * n_chunks + c
    n_steps = pl.num_programs(0) * n_chunks

    def page_copies(page, slot, j):
        lat_cp = pltpu.make_async_copy(lat_hbm.at[pl.ds(pl.multiple_of(page * page_rows, page_rows), page_rows)],
                                       lat_buf.at[slot, j], sem.at[0, slot])
        kr_cp = pltpu.make_async_copy(kr_hbm.at[page, layer], kr_buf.at[slot, j], sem.at[1, slot])
        return lat_cp, kr_cp

    def fetch(at_step, slot):
        for j in range(pages):
            for cp in page_copies(pt_ref[at_step * pages + j], slot, j):
                cp.start()

    @pl.when(step == 0)
    def _():
        fetch(0, 0)

    @pl.when(step + 1 < n_steps)
    def _():
        fetch(step + 1, (step + 1) % 2)

    slot = step % 2
    for j in range(pages):
        for cp in page_copies(0, slot, j):
            cp.wait()

    qlat = qlat_ref[0]
    qr_full = qr_ref[0]
    qr = qr_full[:, :ROPE_DIM]

    def page_latent(j):
        stride = 2 * n_layers
        lo = lat_buf[slot, j, pl.ds(2 * layer, PAGE_SIZE, stride=stride), :]
        hi = lat_buf[slot, j, pl.ds(2 * layer + 1, PAGE_SIZE, stride=stride), :]
        return jnp.concatenate([lo, hi], axis=-1).astype(BF16)

    @pl.when(c == 0)
    def _():
        latn = latn_ref[0]
        s0 = (jnp.sum(qlat.astype(F32) * latn.astype(BF16).astype(F32), -1, keepdims=True)
              + jnp.sum(qr_full.astype(F32) * krn_ref[0].astype(BF16).astype(F32), -1, keepdims=True)) * MLA_SCALE
        m_sc[...] = s0
        l_sc[...] = jnp.ones_like(s0)
        acc_sc[...] = jnp.broadcast_to(latn.astype(BF16).astype(F32), acc_sc.shape)

    lats = [page_latent(j) for j in range(pages)]
    kr_all = jnp.concatenate([kr_buf[slot, j].astype(BF16) for j in range(pages)], axis=1)
    s_all = (_dot_t(qlat, jnp.concatenate(lats, axis=0)) + _dot(qr, kr_all)) * MLA_SCALE
    scores = [s_all[:, j * PAGE_SIZE:(j + 1) * PAGE_SIZE] for j in range(pages)]
    row_max = [jnp.max(sc, -1, keepdims=True) for sc in scores]
    ms = [m_sc[...]]
    for j in range(pages):
        ms.append(jnp.maximum(ms[-1], row_max[j]))
    ps = [jnp.exp(scores[j] - ms[j + 1]) for j in range(pages)]
    corrs = [jnp.exp(ms[j] - ms[j + 1]) for j in range(pages)]
    row_sum = [jnp.sum(p, -1, keepdims=True) for p in ps]
    pvs = [_dot(ps[j].astype(BF16), lats[j]) for j in range(pages)]
    l, acc = l_sc[...], acc_sc[...]
    for j in range(pages):
        l = l * corrs[j] + row_sum[j]
        acc = acc * corrs[j] + pvs[j]
    m_sc[...] = ms[-1]
    l_sc[...] = l
    acc_sc[...] = acc

    @pl.when(c == n_chunks - 1)
    def _():
        o_ref[0] = acc / l


def _mla_decode(qlat, qr, lat_new, kr_new, cache_lat, cache_kr, page_table, layer):
    b, hp, _ = qlat.shape
    n_pages = page_table.shape[1]
    n_phys, _, n_layers, _ = cache_lat.shape
    pages = min(MLA_PAGES_PER_STEP, n_pages)
    assert n_pages % pages == 0
    per_b = lambda i, c, pt: (i, 0, 0)
    page_rows = PAGE_SIZE * n_layers * (KV_LORA // LANES)
    cache_lat = cache_lat.reshape(n_phys * page_rows, LANES)
    cache_kr = jnp.transpose(cache_kr, (0, 2, 3, 1))
    any_spec = pl.BlockSpec(memory_space=pl.ANY)
    grid_spec = pltpu.PrefetchScalarGridSpec(
        num_scalar_prefetch=1,
        grid=(b, n_pages // pages),
        in_specs=[pl.BlockSpec((1, hp, KV_LORA), per_b), pl.BlockSpec((1, hp, LANES), per_b),
                  pl.BlockSpec((1, 1, KV_LORA), per_b), pl.BlockSpec((1, 1, LANES), per_b), any_spec, any_spec],
        out_specs=pl.BlockSpec((1, hp, KV_LORA), per_b),
        scratch_shapes=[pltpu.VMEM((2, pages, page_rows, LANES), F32), pltpu.VMEM((2, pages, ROPE_DIM, PAGE_SIZE), F32),
                        pltpu.SemaphoreType.DMA((2, 2)),
                        pltpu.VMEM((hp, 1), F32), pltpu.VMEM((hp, 1), F32), pltpu.VMEM((hp, KV_LORA), F32)],
    )
    return pl.pallas_call(
        functools.partial(_mla_decode_kernel, layer=layer, n_layers=n_layers),
        grid_spec=grid_spec,
        out_shape=jax.ShapeDtypeStruct((b, hp, KV_LORA), F32),
        compiler_params=_cparams(2),
        name="mla_decode",
    )(page_table.reshape(-1), qlat, qr, lat_new, kr_new, cache_lat, cache_kr)


def _proj_rope_kernel(x_ref, w_ref, c_ref, slo_ref, shi_ref, *out_refs, segs):
    xb = x_ref[...].astype(BF16)
    for c0, n, rope, oi, o0 in segs:
        r = _dot(xb, w_ref[:, c0:c0 + n])
        if rope:
            c, slo, shi = c_ref[...], slo_ref[...], shi_ref[...]
            r = jnp.concatenate([_rope_apply(r[:, a:a + HEAD_DIM], c, slo, shi, HEAD_DIM // 2)
                                 for a in range(0, n, HEAD_DIM)], axis=-1)
        out_refs[oi][:, o0:o0 + n] = r.astype(out_refs[oi].dtype)


def _proj_rope(x, w, tabs, n_tab_blocks, segs, outs):
    m, d = x.shape
    tm = _row_tile(m)
    row = lambda i: (i, 0)
    tab = lambda i: (i % n_tab_blocks, 0)
    return pl.pallas_call(
        functools.partial(_proj_rope_kernel, segs=segs),
        grid=(m // tm,),
        in_specs=[pl.BlockSpec((tm, d), row), pl.BlockSpec(w.shape, lambda i: (0, 0)),
                  pl.BlockSpec((tm, LANES), tab), pl.BlockSpec((tm, LANES), tab), pl.BlockSpec((tm, LANES), tab)],
        out_specs=[pl.BlockSpec((tm, n), row) for n, _ in outs],
        out_shape=[jax.ShapeDtypeStruct((m, n), dt) for n, dt in outs],
        compiler_params=_cparams(1),
        name="proj_rope",
    )(x, w, *tabs)


def _band_attn_kernel(q_ref, kp_ref, kc_ref, vp_ref, vc_ref, o_ref, lse_ref):
    i = pl.program_id(1)
    t = q_ref.shape[1]
    rows = lax.broadcasted_iota(jnp.int32, (t, t), 0)
    cols = lax.broadcasted_iota(jnp.int32, (t, t), 1)
    mask_cur = cols <= rows
    mask_prev = jnp.logical_and(cols >= rows, i > 0)
    for h in range(DIL_KV_HEADS):
        sl = slice(h * HEAD_DIM, (h + 1) * HEAD_DIM)
        q = q_ref[0, :, sl]
        s_c = jnp.where(mask_cur, _dot_t(q, kc_ref[0, :, sl]) * HEAD_SCALE, NEG_INF)
        s_p = jnp.where(mask_prev, _dot_t(q, kp_ref[0, :, sl]) * HEAD_SCALE, NEG_INF)
        m = jnp.maximum(jnp.max(s_c, -1, keepdims=True), jnp.max(s_p, -1, keepdims=True))
        e_c = jnp.exp(s_c - m)
        e_p = jnp.exp(s_p - m)
        l = jnp.sum(e_c, -1, keepdims=True) + jnp.sum(e_p, -1, keepdims=True)
        o = _dot(e_c.astype(BF16), vc_ref[0, :, sl]) + _dot(e_p.astype(BF16), vp_ref[0, :, sl])
        o_ref[0, :, sl] = o / l
        lse_ref[0, :, sl] = jnp.broadcast_to(m + jnp.log(l), (t, HEAD_DIM))


def _band_attn(q, k, v):
    s, l, w = q.shape
    t = DIL_KEYS
    cur = lambda a, i: (a, i, 0)
    prev = lambda a, i: (a, jnp.maximum(i - 1, 0), 0)
    blk = (1, t, w)
    return pl.pallas_call(
        _band_attn_kernel,
        grid=(s, l // t),
        in_specs=[pl.BlockSpec(blk, cur), pl.BlockSpec(blk, prev), pl.BlockSpec(blk, cur),
                  pl.BlockSpec(blk, prev), pl.BlockSpec(blk, cur)],
        out_specs=[pl.BlockSpec(blk, cur), pl.BlockSpec(blk, cur)],
        out_shape=[jax.ShapeDtypeStruct((s, l, w), F32), jax.ShapeDtypeStruct((s, l, w), F32)],
        compiler_params=_cparams(2),
        name="band_attn",
    )(q, k, k, v, v)


def _dil_sample_kernel(q_ref, kn_ref, vn_ref, *refs, bb):
    k_refs, v_refs, o_ref = refs[:N_DIL_GROUPS], refs[N_DIL_GROUPS:2 * N_DIL_GROUPS], refs[2 * N_DIL_GROUPS]
    for j in range(bb):
        outs, lses = [], []
        for g in range(N_DIL_GROUPS):
            q_row = q_ref[pl.ds(j, 1), g * DIL_KV_DIM:(g + 1) * DIL_KV_DIM]
            k_heads, v_heads = [], []
            for h in range(DIL_KV_HEADS):
                sl = slice(h * HEAD_DIM, (h + 1) * HEAD_DIM)
                k_new = jnp.broadcast_to(kn_ref[pl.ds(j, 1), sl], (8, HEAD_DIM))
                v_new = jnp.broadcast_to(vn_ref[pl.ds(j, 1), sl], (8, HEAD_DIM))
                k_heads.append(jnp.concatenate([k_refs[g][j, :, h, :], k_new], 0).astype(BF16))
                v_heads.append(jnp.concatenate([v_refs[g][j, :, h, :], v_new], 0).astype(BF16))
            o, lse = _single_query_attend(q_row, k_heads, v_heads, DIL_KEYS + 1, False)
            outs.append(o)
            lses.append(lse)
        mx = functools.reduce(jnp.maximum, lses)
        ws = [jnp.exp(x - mx) for x in lses]
        mix = sum(o * w for o, w in zip(outs, ws)) / sum(ws)
        o_ref[pl.ds(j, 1), :] = _collapse_heads(mix).astype(o_ref.dtype)


def _dil_sample(q, k_new, v_new, cache_k, cache_v):
    b, wbuf = cache_k.shape[:2]
    bb = 8
    views, specs = [], []
    for window, d in DIL_PATTERNS:
        assert wbuf % (d * DIL_KEYS) == 0 and wbuf >= window
        last = wbuf // d // DIL_KEYS - 1
        specs.append(pl.BlockSpec((bb, DIL_KEYS, None, DIL_KV_HEADS, HEAD_DIM),
                                  functools.partial(lambda i, last_: (i, last_, 0, 0, 0), last_=last)))
        views.append(lambda c, d_=d: c.reshape(b, wbuf // d_, d_, DIL_KV_HEADS, HEAD_DIM))
    row = lambda i: (i, 0)
    return pl.pallas_call(
        functools.partial(_dil_sample_kernel, bb=bb),
        grid=(b // bb,),
        in_specs=[pl.BlockSpec((bb, MIX_HEADS * HEAD_DIM), row), pl.BlockSpec((bb, DIL_KV_DIM), row),
                  pl.BlockSpec((bb, DIL_KV_DIM), row)] + specs + specs,
        out_specs=pl.BlockSpec((bb, DIL_KV_DIM), row),
        out_shape=jax.ShapeDtypeStruct((b, DIL_KV_DIM), BF16),
        compiler_params=_cparams(1),
        name="dil_sample",
    )(q, k_new, v_new, *[f(cache_k) for f in views], *[f(cache_v) for f in views])


ROUTE_GROUP_COL = 0
ROUTE_EXPERT_COL = 8


def _layer_norm(s, g, b):
    mu = jnp.mean(s, -1, keepdims=True)
    d = s - mu
    var = jnp.mean(d * d, -1, keepdims=True)
    return d * lax.rsqrt(var + LN_EPS) * g + b


def _route(x, w, bias):
    logits = _dot(x.astype(BF16), w) + bias
    col = lax.broadcasted_iota(jnp.int32, logits.shape, 1).astype(F32)
    big = float(LANES)
    is_g = col < N_EXPERT_GROUPS
    gl = jnp.where(is_g, logits, NEG_INF)
    gmax = jnp.max(gl, -1, keepdims=True)
    g_sel = jnp.min(jnp.where(gl == gmax, col, big), -1, keepdims=True)
    g_w = 1.0 / jnp.sum(jnp.where(is_g, jnp.exp(gl - gmax), 0.0), -1, keepdims=True)
    lo = ROUTE_EXPERT_COL + EXPERTS_PER_GROUP * g_sel
    el = jnp.where(jnp.logical_and(col >= lo, col < lo + EXPERTS_PER_GROUP), logits, NEG_INF)
    m1 = jnp.max(el, -1, keepdims=True)
    i1 = jnp.min(jnp.where(el == m1, col, big), -1, keepdims=True)
    el2 = jnp.where(col == i1, NEG_INF, el)
    m2 = jnp.max(el2, -1, keepdims=True)
    i2 = jnp.min(jnp.where(el2 == m2, col, big), -1, keepdims=True)
    e2 = jnp.exp(m2 - m1)
    w1 = g_w / (1.0 + e2)
    w2 = g_w * e2 / (1.0 + e2)
    out = jnp.where(col == 0, i1 - ROUTE_EXPERT_COL, 0.0)
    out = jnp.where(col == 1, i2 - ROUTE_EXPERT_COL, out)
    out = jnp.where(col == 2, w1, out)
    return jnp.where(col == 3, w2, out)


def _attn_out_tail(delta, h_ref, g_ref, b_ref, wr_ref, rb_ref, h1_ref, rows_ref, route_ref):
    h1 = _layer_norm(DEEPNORM_ALPHA * h_ref[...] + delta, g_ref[...], b_ref[...])
    h1_ref[...] = h1
    _store_token_rows(rows_ref, h1)
    route_ref[...] = _route(h1, wr_ref[...], rb_ref[...])


def _attn_out_mla_kernel(mix_ref, mem_ref, w_ref, *tail):
    n_mix = mix_ref.shape[1]
    delta = _dot(mix_ref[...], w_ref[0:n_mix, :]) + _dot(mem_ref[...], w_ref[n_mix:, :])
    _attn_out_tail(delta, *tail)


def _attn_out_dil_kernel(o0_ref, o1_ref, o2_ref, l0_ref, l1_ref, l2_ref, mem_ref, w_ref, *tail):
    l0, l1, l2 = l0_ref[...], l1_ref[...], l2_ref[...]
    mx = jnp.maximum(jnp.maximum(l0, l1), l2)
    e0, e1, e2 = jnp.exp(l0 - mx), jnp.exp(l1 - mx), jnp.exp(l2 - mx)
    mix = (o0_ref[...] * e0 + o1_ref[...] * e1 + o2_ref[...] * e2) / (e0 + e1 + e2)
    n_mix = mix.shape[1]
    delta = _dot(mix.astype(BF16), w_ref[0:n_mix, :]) + _dot(mem_ref[...], w_ref[n_mix:, :])
    _attn_out_tail(delta, *tail)


def _attn_out(kernel_fn, parts, w, h, ln_g, ln_b, w_route, r_bias, name):
    m = h.shape[0]
    tm = min(_row_tile(m), 256)
    row = lambda i: (i, 0)
    fixed = lambda i: (0, 0)
    full = lambda a: pl.BlockSpec(a.shape, fixed)
    return pl.pallas_call(
        kernel_fn,
        grid=(m // tm,),
        in_specs=[pl.BlockSpec((tm, p.shape[1]), row) for p in parts]
        + [full(w), pl.BlockSpec((tm, D_MODEL), row), full(ln_g), full(ln_b), full(w_route), full(r_bias)],
        out_specs=[pl.BlockSpec((tm, D_MODEL), row), pl.BlockSpec((tm * ROW_CHUNKS, LANES), row),
                   pl.BlockSpec((tm, LANES), row)],
        out_shape=[jax.ShapeDtypeStruct((m, D_MODEL), F32), jax.ShapeDtypeStruct((m * ROW_CHUNKS, LANES), F32),
                   jax.ShapeDtypeStruct((m, LANES), F32)],
        compiler_params=_cparams(1),
        name=name,
    )(*parts, w, h, ln_g, ln_b, w_route, r_bias)


def _moe_num_tiles(n_tokens):
    worst_rows = 2 * n_tokens + N_EXPERTS * (MOE_TM - 1)
    return -(-worst_rows // MOE_TM)


def _moe_dispatch(route):
    n = route.shape[0]
    ids = route[:, 0:2].astype(jnp.int32).reshape(-1)
    gates = route[:, 2:4].reshape(-1)
    onehot = (ids[:, None] == jnp.arange(N_EXPERTS, dtype=jnp.int32)[None, :]).astype(jnp.int32)
    counts = jnp.sum(onehot, 0)
    rank = jnp.sum((jnp.cumsum(onehot, 0) - onehot) * onehot, 1)
    padded = (counts + MOE_TM - 1) // MOE_TM * MOE_TM
    ends = jnp.cumsum(padded)
    pos = (ends - padded)[ids] + rank
    n_tiles = _moe_num_tiles(n)
    rows = n_tiles * MOE_TM
    packed = jnp.stack([jnp.arange(2 * n, dtype=jnp.int32) // 2, lax.bitcast_convert_type(gates, jnp.int32)], axis=1)
    packed = jnp.zeros((rows, 2), jnp.int32).at[pos].set(packed)
    row_token, row_gate = packed[:, 0], lax.bitcast_convert_type(packed[:, 1], F32)
    tile_start = jnp.arange(n_tiles, dtype=jnp.int32) * MOE_TM
    tile_expert = jnp.minimum(jnp.searchsorted(ends, tile_start, side="right"), N_EXPERTS - 1).astype(jnp.int32)
    n_valid = (ends[-1] // MOE_TM).astype(jnp.int32).reshape(1)
    return tile_expert, n_valid, row_token, row_gate.reshape(rows, 1), pos.astype(jnp.int32)


def _row_copy(src_hbm, src_row, dst_buf, dst_row, sem):
    src = pl.ds(pl.multiple_of(src_row * ROW_CHUNKS, ROW_CHUNKS), ROW_CHUNKS)
    dst = pl.ds(pl.multiple_of(dst_row * ROW_CHUNKS, ROW_CHUNKS), ROW_CHUNKS)
    return pltpu.make_async_copy(src_hbm.at[src], dst_buf.at[dst], sem)


DMA_LOOP_UNROLL = 8


def _moe_ffn_kernel(te_ref, nv_ref, rt_ref, x_hbm, gate_ref, wg_ref, wu_ref, wd_ref, y_ref,
                    xbuf, sem, wg_b, wu_b, wd_b):
    i = pl.program_id(0)
    n_valid = nv_ref[0]
    tm = xbuf.shape[1] // ROW_CHUNKS

    def gather(tile, slot):
        def body(k, carry):
            _row_copy(x_hbm, rt_ref[tile * tm + k], xbuf.at[slot], k, sem.at[slot]).start()
            return carry
        lax.fori_loop(0, tm, body, 0, unroll=DMA_LOOP_UNROLL)

    def gather_wait(slot):
        def body(k, carry):
            _row_copy(x_hbm, 0, xbuf.at[slot], k, sem.at[slot]).wait()
            return carry
        lax.fori_loop(0, tm, body, 0, unroll=DMA_LOOP_UNROLL)

    @pl.when(jnp.logical_and(i == 0, n_valid > 0))
    def _():
        gather(0, 0)

    @pl.when(i + 1 < n_valid)
    def _():
        gather(i + 1, (i + 1) % 2)

    @pl.when(i < n_valid)
    def _():
        new_expert = jnp.logical_or(i == 0, te_ref[i] != te_ref[jnp.maximum(i - 1, 0)])

        @pl.when(new_expert)
        def _():
            wg_b[...] = wg_ref[0].astype(BF16)
            wu_b[...] = wu_ref[0].astype(BF16)
            wd_b[...] = wd_ref[0].astype(BF16)

        slot = i % 2
        gather_wait(slot)
        xb = _load_token_rows(xbuf.at[slot], tm).astype(BF16)
        g = _dot(xb, wg_b[...])
        u = _dot(xb, wu_b[...])
        hid = g / (1.0 + jnp.exp(-g)) * u * gate_ref[...]
        _store_token_rows(y_ref, _dot(hid.astype(BF16), wd_b[...]))

    @pl.when(i >= n_valid)
    def _():
        y_ref[...] = jnp.zeros_like(y_ref)


def _moe_ffn(x, w_g, w_u, w_d, layer, tile_expert, n_valid, row_token, row_gate):
    d = w_g.shape[2]
    hdim = w_g.shape[3]
    rows = row_token.shape[0]
    tm = MOE_TM
    n_tiles = rows // tm
    live = lambda i, te, nv, rt: jnp.minimum(i, jnp.maximum(nv[0] - 1, 0))
    grid_spec = pltpu.PrefetchScalarGridSpec(
        num_scalar_prefetch=3,
        grid=(n_tiles,),
        in_specs=[pl.BlockSpec(memory_space=pl.ANY),
                  pl.BlockSpec((tm, 1), lambda i, te, nv, rt: (live(i, te, nv, rt), 0)),
                  pl.BlockSpec((None, 1, d, hdim), lambda i, te, nv, rt: (layer, te[live(i, te, nv, rt)], 0, 0)),
                  pl.BlockSpec((None, 1, d, hdim), lambda i, te, nv, rt: (layer, te[live(i, te, nv, rt)], 0, 0)),
                  pl.BlockSpec((None, 1, hdim, d), lambda i, te, nv, rt: (layer, te[live(i, te, nv, rt)], 0, 0))],
        out_specs=pl.BlockSpec((tm * ROW_CHUNKS, LANES), lambda i, te, nv, rt: (i, 0)),
        scratch_shapes=[pltpu.VMEM((2, tm * ROW_CHUNKS, LANES), F32), pltpu.SemaphoreType.DMA((2,)),
                        pltpu.VMEM((d, hdim), BF16), pltpu.VMEM((d, hdim), BF16), pltpu.VMEM((hdim, d), BF16)],
    )
    return pl.pallas_call(
        _moe_ffn_kernel,
        grid_spec=grid_spec,
        out_shape=jax.ShapeDtypeStruct((rows * ROW_CHUNKS, LANES), F32),
        compiler_params=_cparams(1),
        name="moe_ffn",
    )(tile_expert, n_valid, row_token, x, row_gate, w_g, w_u, w_d)


def _moe_combine_kernel(pos_ref, h_ref, y_hbm, g_ref, b_ref, o_ref, ybuf, sem, *, tok0):
    i = pl.program_id(0)
    n_steps = pl.num_programs(0)
    tm = h_ref.shape[0]

    def gather(step, slot):
        def body(k, carry):
            base = 2 * (tok0 + step * tm + k)
            _row_copy(y_hbm, pos_ref[base], ybuf.at[slot, 0], k, sem.at[slot]).start()
            _row_copy(y_hbm, pos_ref[base + 1], ybuf.at[slot, 1], k, sem.at[slot]).start()
            return carry
        lax.fori_loop(0, tm, body, 0, unroll=DMA_LOOP_UNROLL)

    def gather_wait(slot):
        def body(k, carry):
            _row_copy(y_hbm, 0, ybuf.at[slot, 0], k, sem.at[slot]).wait()
            _row_copy(y_hbm, 0, ybuf.at[slot, 1], k, sem.at[slot]).wait()
            return carry
        lax.fori_loop(0, tm, body, 0, unroll=DMA_LOOP_UNROLL)

    @pl.when(i == 0)
    def _():
        gather(0, 0)

    @pl.when(i + 1 < n_steps)
    def _():
        gather(i + 1, (i + 1) % 2)

    slot = i % 2
    gather_wait(slot)
    s = DEEPNORM_ALPHA * h_ref[...] + (_load_token_rows(ybuf.at[slot, 0], tm) + _load_token_rows(ybuf.at[slot, 1], tm))
    o_ref[...] = _layer_norm(s, g_ref[...], b_ref[...])


def _moe_combine(pos, h1, y_sorted, ln_g, ln_b, tok0):
    n_rows, d = h1.shape
    tm = min(_row_tile(n_rows), 256)
    grid_spec = pltpu.PrefetchScalarGridSpec(
        num_scalar_prefetch=1,
        grid=(n_rows // tm,),
        in_specs=[pl.BlockSpec((tm, d), lambda i, p: (i, 0)),
                  pl.BlockSpec(memory_space=pl.ANY),
                  pl.BlockSpec((1, d), lambda i, p: (0, 0)), pl.BlockSpec((1, d), lambda i, p: (0, 0))],
        out_specs=pl.BlockSpec((tm, d), lambda i, p: (i, 0)),
        scratch_shapes=[pltpu.VMEM((2, 2, tm * ROW_CHUNKS, LANES), F32), pltpu.SemaphoreType.DMA((2,))],
    )
    return pl.pallas_call(
        functools.partial(_moe_combine_kernel, tok0=tok0),
        grid_spec=grid_spec,
        out_shape=jax.ShapeDtypeStruct((n_rows, d), F32),
        compiler_params=_cparams(1),
        name="moe_combine",
    )(pos, h1, y_sorted, ln_g, ln_b)


def _router_weights(w_rg, b_rg, w_re, b_re):
    d = w_rg.shape[0]
    w = jnp.zeros((d, LANES), F32)
    w = w.at[:, ROUTE_GROUP_COL:ROUTE_GROUP_COL + N_EXPERT_GROUPS].set(w_rg)
    w = w.at[:, ROUTE_EXPERT_COL:ROUTE_EXPERT_COL + N_EXPERTS].set(w_re)
    bias = jnp.zeros((1, LANES), F32)
    bias = bias.at[0, ROUTE_GROUP_COL:ROUTE_GROUP_COL + N_EXPERT_GROUPS].set(b_rg)
    bias = bias.at[0, ROUTE_EXPERT_COL:ROUTE_EXPERT_COL + N_EXPERTS].set(b_re)
    return w.astype(BF16), bias


def _moe_block(prompt, sample, w_g, w_u, w_d, layer, ln_g, ln_b):
    (h1_p, rows_p, route_p), (h1_s, rows_s, route_s) = prompt, sample
    rows = jnp.concatenate([rows_p, rows_s], axis=0)
    route = jnp.concatenate([route_p, route_s], axis=0)
    tile_expert, n_valid, row_token, row_gate, pos = _moe_dispatch(route)
    y_sorted = _moe_ffn(rows, w_g, w_u, w_d, layer, tile_expert, n_valid, row_token, row_gate)
    out_p = _moe_combine(pos, h1_p, y_sorted, ln_g, ln_b, 0)
    out_s = _moe_combine(pos, h1_s, y_sorted, ln_g, ln_b, h1_p.shape[0])
    return out_p, out_s


def kernel(x_prompt, x_sample, cache_mla_latent, cache_mla_krope, cache_swa_k, cache_swa_v, cache_mem_k, cache_mem_v, page_table, mem_prompt, w_in_a, q_norm_a, kv_norm_a, w_q_up_a, w_kv_up_a, w_out_a, w_in_b, w_kv_shared, w_out_b, w_mem_kv, ln1_g, ln1_b, ln2_g, ln2_b, w_router_group, b_router_group, w_router_expert, b_router_expert, w_exp_gate, w_exp_up, w_exp_down):
    bp, tp, d = x_prompt.shape
    bs, ts, _ = x_sample.shape
    assert ts == 1 and w_in_a.shape[0] == 1 and w_in_b.shape[0] == 1
    n_p, n_s = bp * tp, bs * ts
    n_mem = mem_prompt.shape[1]
    wbuf = cache_swa_k.shape[1]
    past_len = page_table.shape[1] * PAGE_SIZE
    assert wbuf == DIL_PATTERNS[-1][0]

    pos_p = jnp.arange(tp, dtype=jnp.int32)
    pos_s = jnp.full((n_s,), past_len, jnp.int32)
    tm_p = _row_tile(n_p)
    assert tp % tm_p == 0
    tab_blocks_p = tp // tm_p
    tabs_r_p = _rope_tables(pos_p, ROPE_DIM, LANES)
    tabs_r_s = _rope_tables(pos_s, ROPE_DIM, LANES)
    tabs_h_p = _rope_tables(pos_p, HEAD_DIM, LANES)
    tabs_h_s = _rope_tables(pos_s, HEAD_DIM, LANES)

    hp = x_prompt.reshape(n_p, d)
    hs = x_sample.reshape(n_s, d)
    ln = lambda a, l: a[l].reshape(1, d)

    mem2d = mem_prompt.reshape(bp * n_mem, d)
    mkv = [_matmul(mem2d, w_mem_kv[l].astype(BF16)).reshape(bp, n_mem, 2 * MEM_DIM) for l in range(DEPTH)]

    w_in = w_in_a[0]
    o_kr = Q_LORA + KV_LORA
    w_in = jnp.concatenate([w_in[:, :o_kr + ROPE_DIM], jnp.zeros((d, LANES - ROPE_DIM), F32),
                            w_in[:, o_kr + ROPE_DIM:]], axis=1).astype(BF16)
    w_q = jnp.concatenate([w_q_up_a[0], jnp.zeros((Q_LORA, MIX_HEADS, MLA_QK - NOPE_DIM - ROPE_DIM), F32)], axis=-1)
    w_q = w_q.reshape(Q_LORA, MIX_HEADS * MLA_QK).astype(BF16)
    w_kv = w_kv_up_a[0].reshape(KV_LORA, MIX_HEADS * (NOPE_DIM + V_DIM)).astype(BF16)
    w_uk_t = jnp.transpose(w_kv_up_a[0][..., :NOPE_DIM], (1, 2, 0)).astype(BF16)
    w_uv = jnp.transpose(w_kv_up_a[0][..., NOPE_DIM:], (1, 0, 2)).astype(BF16)
    q_g, kv_g = q_norm_a[0].reshape(1, Q_LORA), kv_norm_a[0].reshape(1, KV_LORA)

    cq_p, lat_p, kr_p, qm_p = _mla_in_proj(hp, w_in, q_g, kv_g, tabs_r_p, tab_blocks_p)
    cq_s, lat_s, kr_s, qm_s = _mla_in_proj(hs, w_in, q_g, kv_g, tabs_r_s, 1)
    q_p = _q_up(cq_p, w_q, tabs_r_p, tab_blocks_p)
    q_s = _q_up(cq_s, w_q, tabs_r_s, 1)

    k_full, v_full = _kv_up(lat_p, kr_p, w_kv)
    mix_p, shifted_k, shifted_v = _mla_flash(q_p.reshape(bp, tp, -1), k_full.reshape(bp, tp, -1),
                                             v_full.reshape(bp, tp, -1), cache_swa_k, cache_swa_v)
    mix_p = mix_p.reshape(n_p, -1)

    hpad = 16
    q_s3 = q_s.reshape(n_s, MIX_HEADS, MLA_QK)
    qlat = _head_matmul(jnp.transpose(q_s3[:, :, :NOPE_DIM], (1, 0, 2)), w_uk_t, BF16)
    qlat = jnp.pad(jnp.transpose(qlat, (1, 0, 2)), ((0, 0), (0, hpad - MIX_HEADS), (0, 0)))
    qr = jnp.pad(q_s3[:, :, NOPE_DIM:], ((0, 0), (0, hpad - MIX_HEADS), (0, 0)))
    o_lat = _mla_decode(qlat, qr, lat_s.reshape(n_s, 1, KV_LORA), kr_s.reshape(n_s, 1, LANES),
                        cache_mla_latent, cache_mla_krope, page_table, 0)
    mix_s = _head_matmul(jnp.transpose(o_lat[:, :MIX_HEADS], (1, 0, 2)), w_uv, BF16)
    mix_s = jnp.transpose(mix_s, (1, 0, 2)).reshape(n_s, MIX_HEADS * V_DIM)

    mem_p = _mem_attn_prompt(qm_p.reshape(bp, tp, MEM_DIM), mkv[0]).reshape(n_p, MEM_DIM)
    mem_s = _mem_attn_sample(qm_s, cache_mem_k, cache_mem_v, 0)

    router = [_router_weights(w_router_group[l], b_router_group[l], w_router_expert[l], b_router_expert[l])
              for l in range(DEPTH)]
    w_out = w_out_a[0].astype(BF16)
    out_p = _attn_out(_attn_out_mla_kernel, [mix_p, mem_p], w_out, hp, ln(ln1_g, 0), ln(ln1_b, 0),
                      *router[0], name="attn_out_mla")
    out_s = _attn_out(_attn_out_mla_kernel, [mix_s, mem_s], w_out, hs, ln(ln1_g, 0), ln(ln1_b, 0),
                      *router[0], name="attn_out_mla")
    hp, hs = _moe_block(out_p, out_s, w_exp_gate, w_exp_up, w_exp_down, 0, ln(ln2_g, 0), ln(ln2_b, 0))

    n_q = MIX_HEADS * HEAD_DIM
    w_b = w_in_b[0].astype(BF16)
    w_kvs = w_kv_shared.astype(BF16)
    q_segs = tuple((g * DIL_KV_DIM, DIL_KV_DIM, True, 0, g * DIL_KV_DIM) for g in range(N_DIL_GROUPS))
    q_segs += ((n_q, MEM_DIM, False, 1, 0),)
    kv_segs = ((0, DIL_KV_DIM, True, 0, 0), (DIL_KV_DIM, DIL_KV_DIM, False, 1, 0))
    qd_p, qm_p = _proj_rope(hp, w_b, tabs_h_p, tab_blocks_p, q_segs, [(n_q, BF16), (MEM_DIM, BF16)])
    qd_s, qm_s = _proj_rope(hs, w_b, tabs_h_s, 1, q_segs, [(n_q, F32), (MEM_DIM, BF16)])
    k_p, v_p = _proj_rope(hp, w_kvs, tabs_h_p, tab_blocks_p, kv_segs, [(DIL_KV_DIM, F32), (DIL_KV_DIM, F32)])
    k_s, v_s = _proj_rope(hs, w_kvs, tabs_h_s, 1, kv_segs, [(DIL_KV_DIM, F32), (DIL_KV_DIM, F32)])

    k_pb, v_pb = k_p.astype(BF16), v_p.astype(BF16)
    outs, lses = [], []
    for g, (_, dil) in enumerate(DIL_PATTERNS):
        def split(a):
            a = a.reshape(bp, tp // dil, dil, DIL_KV_DIM)
            return jnp.transpose(a, (0, 2, 1, 3)).reshape(bp * dil, tp // dil, DIL_KV_DIM)

        def merge(a):
            a = a.reshape(bp, dil, tp // dil, DIL_KV_DIM)
            return jnp.transpose(a, (0, 2, 1, 3)).reshape(n_p, DIL_KV_DIM)

        o_g, lse_g = _band_attn(split(qd_p[:, g * DIL_KV_DIM:(g + 1) * DIL_KV_DIM]), split(k_pb), split(v_pb))
        outs.append(merge(o_g))
        lses.append(merge(lse_g))

    mix_s = _dil_sample(qd_s, k_s, v_s, cache_swa_k, cache_swa_v)
    mem_p = _mem_attn_prompt(qm_p.reshape(bp, tp, MEM_DIM), mkv[1]).reshape(n_p, MEM_DIM)
    mem_s = _mem_attn_sample(qm_s, cache_mem_k, cache_mem_v, 1)

    w_out = w_out_b[0].astype(BF16)
    out_p = _attn_out(_attn_out_dil_kernel, outs + lses + [mem_p], w_out, hp, ln(ln1_g, 1), ln(ln1_b, 1),
                      *router[1], name="attn_out_dil")
    out_s = _attn_out(_attn_out_mla_kernel, [mix_s, mem_s], w_out, hs, ln(ln1_g, 1), ln(ln1_b, 1),
                      *router[1], name="attn_out_mla")
    hp, hs = _moe_block(out_p, out_s, w_exp_gate, w_exp_up, w_exp_down, 1, ln(ln2_g, 1), ln(ln2_b, 1))

    kv4 = lambda a, n: a.reshape(n, -1, DIL_KV_HEADS, HEAD_DIM)
    new_k_s = _swa_set_newest(shifted_k, kv4(k_s, bs))
    new_v_s = _swa_set_newest(shifted_v, kv4(v_s, bs))
    keep_p = min(wbuf, tp)
    k_p4, v_p4 = kv4(k_p, bp), kv4(v_p, bp)
    mem_k = jnp.stack([m[:, :, :MEM_DIM].reshape(bp, n_mem, MEM_HEADS, HEAD_DIM) for m in mkv], axis=2)
    mem_v = jnp.stack([m[:, :, MEM_DIM:].reshape(bp, n_mem, MEM_HEADS, HEAD_DIM) for m in mkv], axis=2)
    return (hp.reshape(bp, tp, d), hs.reshape(bs, ts, d),
            lat_p.reshape(bp, tp, 1, KV_LORA), kr_p[:, :ROPE_DIM].reshape(bp, tp, 1, ROPE_DIM),
            lat_s.reshape(bs, ts, 1, KV_LORA), kr_s[:, :ROPE_DIM].reshape(bs, ts, 1, ROPE_DIM),
            k_p4[:, tp - keep_p:], v_p4[:, tp - keep_p:], new_k_s, new_v_s, mem_k, mem_v)
```

```python
import functools

import jax
import jax.numpy as jnp
import numpy as np
from jax import lax
from jax.experimental import pallas as pl
from jax.experimental.pallas import tpu as pltpu

BF16 = jnp.bfloat16
F32 = jnp.float32

D_MODEL = 2048
DEPTH = 2
HEAD_DIM = 128
MIX_HEADS = 12
MEM_HEADS = 4
MEM_DIM = MEM_HEADS * HEAD_DIM
Q_LORA = 512
KV_LORA = 256
NOPE_DIM = 128
ROPE_DIM = 64
V_DIM = 128
PAGE_SIZE = 128
DIL_PATTERNS = ((128, 1), (512, 4), (2048, 16))
N_DIL_GROUPS = len(DIL_PATTERNS)
DIL_KV_HEADS = MIX_HEADS // N_DIL_GROUPS
DIL_KV_DIM = DIL_KV_HEADS * HEAD_DIM
DIL_KEYS = 128
N_EXPERT_GROUPS = 4
EXPERTS_PER_GROUP = 8
N_EXPERTS = N_EXPERT_GROUPS * EXPERTS_PER_GROUP
EXPERT_HIDDEN = 512
ROPE_THETA = 10000.0
LN_EPS = 1e-5
RMS_EPS = 1e-6
DEEPNORM_ALPHA = (2.0 * DEPTH) ** 0.25
MLA_SCALE = (NOPE_DIM + ROPE_DIM) ** -0.5
HEAD_SCALE = HEAD_DIM ** -0.5
MLA_EXP2_SCALE = MLA_SCALE * float(np.log2(np.e))

LANES = 128
MLA_QK = 2 * LANES
VMEM_LIMIT = 56 * 1024 * 1024
MOE_TM = 256
MLA_PAGES_PER_STEP = 16
NEG_INF = float("-inf")
ROW_CHUNKS = D_MODEL // LANES


def _cparams(n_grid):
    return pltpu.CompilerParams(dimension_semantics=("arbitrary",) * n_grid, vmem_limit_bytes=VMEM_LIMIT)


def _row_tile(m):
    for t in (512, 256, 128, 64, 32, 16, 8):
        if m % t == 0:
            return t
    raise ValueError(f"row count {m} not a multiple of 8")


def _load_token_rows(ref, n):
    return jnp.concatenate([ref[pl.ds(c, n, stride=ROW_CHUNKS), :] for c in range(ROW_CHUNKS)], axis=-1)


def _store_token_rows(ref, val):
    n = val.shape[0]
    for c in range(ROW_CHUNKS):
        ref[pl.ds(c, n, stride=ROW_CHUNKS), :] = val[:, c * LANES:(c + 1) * LANES]


def _dot(a, b):
    return jnp.dot(a, b, preferred_element_type=F32)


def _dot_t(a, b):
    return lax.dot_general(a, b, (((1,), (1,)), ((), ())), preferred_element_type=F32)


def _rope_tables(pos, dim, width):
    half = dim // 2
    inv_freq = ROPE_THETA ** (-jnp.arange(0, dim, 2, dtype=F32) / dim)
    ang = pos.astype(F32)[:, None] * inv_freq[None, :]
    cos, sin = jnp.cos(ang), jnp.sin(ang)
    z = jnp.zeros((pos.shape[0], width - dim), F32)
    zh = jnp.zeros_like(cos)
    c = jnp.concatenate([cos, cos, z], axis=-1)
    s_lo = jnp.concatenate([-sin, zh, z], axis=-1)
    s_hi = jnp.concatenate([zh, sin, z], axis=-1)
    return c, s_lo, s_hi


def _rope_apply(x, c, s_lo, s_hi, half):
    width = x.shape[-1]
    return x * c + pltpu.roll(x, width - half, 1) * s_lo + pltpu.roll(x, half, 1) * s_hi


def _mm_kernel(x_ref, w_ref, o_ref):
    o_ref[...] = _dot(x_ref[...].astype(BF16), w_ref[...]).astype(o_ref.dtype)


def _matmul(x, w, out_dtype=F32):
    m, k = x.shape
    n = w.shape[1]
    tm = _row_tile(m)
    return pl.pallas_call(
        _mm_kernel,
        grid=(m // tm,),
        in_specs=[pl.BlockSpec((tm, k), lambda i: (i, 0)), pl.BlockSpec((k, n), lambda i: (0, 0))],
        out_specs=pl.BlockSpec((tm, n), lambda i: (i, 0)),
        out_shape=jax.ShapeDtypeStruct((m, n), out_dtype),
        compiler_params=_cparams(1),
        name="matmul",
    )(x, w)


def _bmm_kernel(x_ref, w_ref, o_ref):
    o_ref[0] = _dot(x_ref[0].astype(BF16), w_ref[0]).astype(o_ref.dtype)


def _head_matmul(x, w, out_dtype):
    h, m, k = x.shape
    n = w.shape[2]
    return pl.pallas_call(
        _bmm_kernel,
        grid=(h,),
        in_specs=[pl.BlockSpec((1, m, k), lambda i: (i, 0, 0)), pl.BlockSpec((1, k, n), lambda i: (i, 0, 0))],
        out_specs=pl.BlockSpec((1, m, n), lambda i: (i, 0, 0)),
        out_shape=jax.ShapeDtypeStruct((h, m, n), out_dtype),
        compiler_params=_cparams(1),
        name="head_matmul",
    )(x, w)


def _mla_in_kernel(x_ref, w_ref, qg_ref, kvg_ref, c_ref, slo_ref, shi_ref, cq_ref, lat_ref, kr_ref, qm_ref):
    xb = x_ref[...].astype(BF16)
    c_q = _dot(xb, w_ref[:, 0:Q_LORA])
    cq_ref[...] = (c_q * lax.rsqrt(jnp.mean(c_q * c_q, -1, keepdims=True) + RMS_EPS) * qg_ref[...]).astype(cq_ref.dtype)
    o = Q_LORA
    c_kv = _dot(xb, w_ref[:, o:o + KV_LORA])
    lat_ref[...] = c_kv * lax.rsqrt(jnp.mean(c_kv * c_kv, -1, keepdims=True) + RMS_EPS) * kvg_ref[...]
    o += KV_LORA
    k_rope = _dot(xb, w_ref[:, o:o + LANES])
    kr_ref[...] = _rope_apply(k_rope, c_ref[...], slo_ref[...], shi_ref[...], ROPE_DIM // 2)
    o += LANES
    qm_ref[...] = _dot(xb, w_ref[:, o:o + MEM_DIM]).astype(qm_ref.dtype)


def _mla_in_proj(x, w, q_g, kv_g, tabs, n_tab_blocks):
    m, d = x.shape
    tm = _row_tile(m)
    n = w.shape[1]
    row = lambda i: (i, 0)
    fixed = lambda i: (0, 0)
    tab = lambda i: (i % n_tab_blocks, 0)
    return pl.pallas_call(
        _mla_in_kernel,
        grid=(m // tm,),
        in_specs=[pl.BlockSpec((tm, d), row), pl.BlockSpec((d, n), fixed),
                  pl.BlockSpec((1, Q_LORA), fixed), pl.BlockSpec((1, KV_LORA), fixed),
                  pl.BlockSpec((tm, LANES), tab), pl.BlockSpec((tm, LANES), tab), pl.BlockSpec((tm, LANES), tab)],
        out_specs=[pl.BlockSpec((tm, Q_LORA), row), pl.BlockSpec((tm, KV_LORA), row),
                   pl.BlockSpec((tm, LANES), row), pl.BlockSpec((tm, MEM_DIM), row)],
        out_shape=[jax.ShapeDtypeStruct((m, Q_LORA), BF16), jax.ShapeDtypeStruct((m, KV_LORA), F32),
                   jax.ShapeDtypeStruct((m, LANES), F32), jax.ShapeDtypeStruct((m, MEM_DIM), BF16)],
        compiler_params=_cparams(1),
        name="mla_in_proj",
    )(x, w, q_g, kv_g, *tabs)


def _q_up_kernel(cq_ref, w_ref, c_ref, slo_ref, shi_ref, q_ref):
    cq = cq_ref[...]
    c, slo, shi = c_ref[...], slo_ref[...], shi_ref[...]
    for h in range(MIX_HEADS):
        r = _dot(cq, w_ref[:, h * MLA_QK:(h + 1) * MLA_QK])
        q_ref[:, h * MLA_QK:h * MLA_QK + LANES] = r[:, :LANES].astype(q_ref.dtype)
        roped = _rope_apply(r[:, LANES:], c, slo, shi, ROPE_DIM // 2)
        q_ref[:, h * MLA_QK + LANES:(h + 1) * MLA_QK] = roped.astype(q_ref.dtype)


def _q_up(cq, w, tabs, n_tab_blocks):
    m, k = cq.shape
    tm = _row_tile(m)
    n = w.shape[1]
    row = lambda i: (i, 0)
    fixed = lambda i: (0, 0)
    tab = lambda i: (i % n_tab_blocks, 0)
    return pl.pallas_call(
        _q_up_kernel,
        grid=(m // tm,),
        in_specs=[pl.BlockSpec((tm, k), row), pl.BlockSpec((k, n), fixed),
                  pl.BlockSpec((tm, LANES), tab), pl.BlockSpec((tm, LANES), tab), pl.BlockSpec((tm, LANES), tab)],
        out_specs=pl.BlockSpec((tm, n), row),
        out_shape=jax.ShapeDtypeStruct((m, n), BF16),
        compiler_params=_cparams(1),
        name="mla_q_up",
    )(cq, w, *tabs)


def _kv_up_kernel(lat_ref, kr_ref, w_ref, k_ref, v_ref):
    lat = lat_ref[...].astype(BF16)
    kr = kr_ref[...].astype(BF16)
    for h in range(MIX_HEADS):
        r = _dot(lat, w_ref[:, h * 2 * LANES:(h + 1) * 2 * LANES])
        k_ref[:, h * MLA_QK:h * MLA_QK + LANES] = r[:, :LANES].astype(BF16)
        k_ref[:, h * MLA_QK + LANES:(h + 1) * MLA_QK] = kr
        v_ref[:, h * V_DIM:(h + 1) * V_DIM] = r[:, LANES:].astype(BF16)


def _kv_up(lat, kr, w):
    m = lat.shape[0]
    tm = _row_tile(m)
    row = lambda i: (i, 0)
    return pl.pallas_call(
        _kv_up_kernel,
        grid=(m // tm,),
        in_specs=[pl.BlockSpec((tm, KV_LORA), row), pl.BlockSpec((tm, LANES), row),
                  pl.BlockSpec(w.shape, lambda i: (0, 0))],
        out_specs=[pl.BlockSpec((tm, MIX_HEADS * MLA_QK), row), pl.BlockSpec((tm, MIX_HEADS * V_DIM), row)],
        out_shape=[jax.ShapeDtypeStruct((m, MIX_HEADS * MLA_QK), BF16),
                   jax.ShapeDtypeStruct((m, MIX_HEADS * V_DIM), BF16)],
        compiler_params=_cparams(1),
        name="mla_kv_up",
    )(lat, kr, w)


SWA_SHIFT_BATCH = 2
SWA_SHIFT_BUFFERS = 3


def _swa_shift_step(step, n_steps, srcs, dsts, buf, in_sem, out_sem):
    nbuf, bb = buf.shape[:2]
    b, w = srcs[0].shape[:2]
    per = b // bb
    n = per * len(srcs)
    active = step % 3 != 2
    t = jnp.where(active, (step // 3) * 2 + step % 3, n)

    def start_in(c, slot):
        for a, src in enumerate(srcs):
            @pl.when(jnp.logical_and(c >= a * per, c < (a + 1) * per))
            def _():
                pltpu.make_async_copy(src.at[pl.ds((c - a * per) * bb, bb)], buf.at[slot], in_sem.at[slot]).start()

    def out_copies(dst, row0, slot):
        return (pltpu.make_async_copy(buf.at[slot, :, pl.ds(1, w - 1)],
                                      dst.at[pl.ds(row0, bb), pl.ds(0, w - 1)], out_sem.at[slot]),
                pltpu.make_async_copy(buf.at[slot, :, pl.ds(w - 1, 1)],
                                      dst.at[pl.ds(row0, bb), pl.ds(w - 1, 1)], out_sem.at[slot]))

    def start_out(c, slot):
        for a, dst in enumerate(dsts):
            @pl.when(jnp.logical_and(c >= a * per, c < (a + 1) * per))
            def _():
                for cp in out_copies(dst, (c - a * per) * bb, slot):
                    cp.start()

    @pl.when(t == 0)
    def _():
        for c in range(min(nbuf - 1, n)):
            start_in(c, c)

    @pl.when(t < n)
    def _():
        slot = t % nbuf
        pltpu.make_async_copy(srcs[0].at[pl.ds(0, bb)], buf.at[slot], in_sem.at[slot]).wait()
        start_out(t, slot)

    @pl.when(jnp.logical_and(t >= 1, t < n))
    def _():
        for cp in out_copies(dsts[0], 0, (t - 1) % nbuf):
            cp.wait()

    @pl.when(t + nbuf - 1 < n)
    def _():
        start_in(t + nbuf - 1, (t + nbuf - 1) % nbuf)

    @pl.when(step == n_steps - 1)
    def _():
        for cp in out_copies(dsts[0], 0, (n - 1) % nbuf):
            cp.wait()


def _swa_set_newest_kernel(shifted_hbm, new_hbm, o_hbm, sem):
    del shifted_hbm
    w = o_hbm.shape[1]
    cp = pltpu.make_async_copy(new_hbm, o_hbm.at[:, pl.ds(w - 1, 1)], sem)
    cp.start()
    cp.wait()


def _swa_set_newest(shifted, new):
    any_spec = pl.BlockSpec(memory_space=pl.ANY)
    return pl.pallas_call(
        _swa_set_newest_kernel,
        in_specs=[any_spec, any_spec],
        out_specs=any_spec,
        out_shape=jax.ShapeDtypeStruct(shifted.shape, shifted.dtype),
        scratch_shapes=[pltpu.SemaphoreType.DMA(())],
        input_output_aliases={0: 0},
        name="swa_set_newest",
    )(shifted, new)


def _mla_flash_kernel(q_ref, k_ref, v_ref, ck_hbm, cv_hbm, o_ref, ok_hbm, ov_hbm, buf, in_sem, out_sem, *, tq):
    i = pl.program_id(2)
    step = (pl.program_id(0) * pl.num_programs(1) + pl.program_id(1)) * pl.num_programs(2) + i
    n_steps = pl.num_programs(0) * pl.num_programs(1) * pl.num_programs(2)
    _swa_shift_step(step, n_steps, (ck_hbm, cv_hbm), (ok_hbm, ov_hbm), buf, in_sem, out_sem)
    q = q_ref[0]

    def step(j, carry, masked):
        m, l, acc = carry
        start = pl.multiple_of(j * tq, tq)
        s = _dot_t(q, k_ref[0, pl.ds(start, tq), :])
        if masked:
            rows = lax.broadcasted_iota(jnp.int32, s.shape, 0)
            cols = lax.broadcasted_iota(jnp.int32, s.shape, 1)
            s = jnp.where(cols <= rows, s, NEG_INF)
        m_new = jnp.maximum(m, jnp.max(s, -1, keepdims=True))
        corr = jnp.exp2((m - m_new) * MLA_EXP2_SCALE)
        p = jnp.exp2((s - m_new) * MLA_EXP2_SCALE)
        l = l * corr + jnp.sum(p, -1, keepdims=True)
        acc = acc * corr + _dot(p.astype(BF16), v_ref[0, pl.ds(start, tq), :])
        return m_new, l, acc

    init = (jnp.full((tq, 1), NEG_INF, F32), jnp.zeros((tq, 1), F32), jnp.zeros((tq, V_DIM), F32))
    carry = lax.fori_loop(0, i, lambda j, c: step(j, c, False), init)
    _, l, acc = step(i, carry, True)
    o_ref[0] = (acc / l).astype(o_ref.dtype)


def _mla_flash(q, k, v, cache_k, cache_v):
    b, t, _ = q.shape
    tq = _row_tile(t)
    grid = (b, MIX_HEADS, t // tq)
    bs = cache_k.shape[0]
    bb = SWA_SHIFT_BATCH
    n_steps = grid[0] * grid[1] * grid[2]
    assert bs % bb == 0 and n_steps % 3 == 0 and 2 * (bs // bb) <= (n_steps // 3) * 2
    any_spec = pl.BlockSpec(memory_space=pl.ANY)
    return pl.pallas_call(
        functools.partial(_mla_flash_kernel, tq=tq),
        grid=grid,
        in_specs=[pl.BlockSpec((1, tq, MLA_QK), lambda b_, h, i: (b_, i, h)),
                  pl.BlockSpec((1, t, MLA_QK), lambda b_, h, i: (b_, 0, h)),
                  pl.BlockSpec((1, t, V_DIM), lambda b_, h, i: (b_, 0, h)), any_spec, any_spec],
        out_specs=[pl.BlockSpec((1, tq, V_DIM), lambda b_, h, i: (b_, i, h)), any_spec, any_spec],
        out_shape=[jax.ShapeDtypeStruct((b, t, MIX_HEADS * V_DIM), BF16),
                   jax.ShapeDtypeStruct(cache_k.shape, cache_k.dtype), jax.ShapeDtypeStruct(cache_v.shape, cache_v.dtype)],
        scratch_shapes=[pltpu.VMEM((SWA_SHIFT_BUFFERS, bb) + cache_k.shape[1:], cache_k.dtype),
                        pltpu.SemaphoreType.DMA((SWA_SHIFT_BUFFERS,)), pltpu.SemaphoreType.DMA((SWA_SHIFT_BUFFERS,))],
        compiler_params=_cparams(3),
        name="mla_flash",
    )(q, k, v, cache_k, cache_v)


def _mem_attn_kernel(q_ref, k_ref, v_ref, o_ref):
    for h in range(MEM_HEADS):
        sl = slice(h * HEAD_DIM, (h + 1) * HEAD_DIM)
        s = _dot_t(q_ref[0, :, sl], k_ref[0, :, sl].astype(BF16)) * HEAD_SCALE
        p = jnp.exp(s - jnp.max(s, -1, keepdims=True))
        p = p / jnp.sum(p, -1, keepdims=True)
        o_ref[0, :, sl] = _dot(p.astype(BF16), v_ref[0, :, sl].astype(BF16)).astype(o_ref.dtype)


def _mem_attn_prompt(q, mkv):
    b, t, _ = q.shape
    n_mem = mkv.shape[1]
    tq = _row_tile(t)
    return pl.pallas_call(
        _mem_attn_kernel,
        grid=(b, t // tq),
        in_specs=[pl.BlockSpec((1, tq, MEM_DIM), lambda b_, i: (b_, i, 0)),
                  pl.BlockSpec((1, n_mem, MEM_DIM), lambda b_, i: (b_, 0, 0)),
                  pl.BlockSpec((1, n_mem, MEM_DIM), lambda b_, i: (b_, 0, 1))],
        out_specs=pl.BlockSpec((1, tq, MEM_DIM), lambda b_, i: (b_, i, 0)),
        out_shape=jax.ShapeDtypeStruct((b, t, MEM_DIM), BF16),
        compiler_params=_cparams(2),
        name="mem_attn_prompt",
    )(q, mkv, mkv)


def _head_rows(width, rows=8):
    r = lax.broadcasted_iota(jnp.int32, (rows, width), 0)
    c = lax.broadcasted_iota(jnp.int32, (rows, width), 1)
    return (c // HEAD_DIM) == r


def _single_query_attend(q_row, k_heads, v_heads, valid_rows, normalise_first):
    n = k_heads[0].shape[0]
    row_id = lax.broadcasted_iota(jnp.int32, (8, HEAD_DIM), 0)
    s = None
    for h, k_h in enumerate(k_heads):
        q_h = jnp.broadcast_to(q_row[:, h * HEAD_DIM:(h + 1) * HEAD_DIM], (8, HEAD_DIM))
        s_h = _dot_t(k_h, jnp.where(row_id == h, q_h, 0.0).astype(BF16))
        s = s_h if s is None else s + s_h
    s = s * HEAD_SCALE
    if valid_rows < n:
        s = jnp.where(lax.broadcasted_iota(jnp.int32, s.shape, 0) < valid_rows, s, NEG_INF)
    m = jnp.max(s, 0, keepdims=True)
    e = jnp.exp(s - m)
    l = jnp.sum(e, 0, keepdims=True)
    eb = (e / l if normalise_first else e).astype(BF16)
    o = jnp.concatenate(
        [lax.dot_general(eb, v_h, (((0,), (0,)), ((), ())), preferred_element_type=F32) for v_h in v_heads], axis=-1)
    eye = lax.broadcasted_iota(jnp.int32, (8, 8), 0) == lax.broadcasted_iota(jnp.int32, (8, 8), 1)
    lse = jnp.sum(jnp.where(eye, jnp.broadcast_to(m + jnp.log(l), (8, 8)), 0.0), -1, keepdims=True)
    if normalise_first:
        return o, lse
    l_col = jnp.sum(jnp.where(eye, jnp.broadcast_to(l, (8, 8)), 0.0), -1, keepdims=True)
    return o / l_col, lse


def _collapse_heads(o):
    return jnp.sum(jnp.where(_head_rows(o.shape[1]), o, 0.0), 0, keepdims=True)


def _mem_attn_sample_kernel(q_ref, k_ref, v_ref, o_ref, *, bb, layer, n_layers):
    stride = n_layers * MEM_HEADS
    n_mem = k_ref.shape[1] // stride
    for j in range(bb):
        k = [k_ref[j, pl.ds(layer * MEM_HEADS + h, n_mem, stride=stride), :].astype(BF16) for h in range(MEM_HEADS)]
        v = [v_ref[j, pl.ds(layer * MEM_HEADS + h, n_mem, stride=stride), :].astype(BF16) for h in range(MEM_HEADS)]
        o, _ = _single_query_attend(q_ref[pl.ds(j, 1), :].astype(F32), k, v, n_mem, True)
        o_ref[pl.ds(j, 1), :] = _collapse_heads(o).astype(o_ref.dtype)


def _mem_attn_sample(q, cache_k, cache_v, layer):
    b = q.shape[0]
    n_mem, n_layers = cache_k.shape[1:3]
    bb = 8
    rows = n_mem * n_layers * MEM_HEADS
    cache_k, cache_v = cache_k.reshape(b, rows, HEAD_DIM), cache_v.reshape(b, rows, HEAD_DIM)
    cspec = pl.BlockSpec((bb, rows, HEAD_DIM), lambda i: (i, 0, 0))
    return pl.pallas_call(
        functools.partial(_mem_attn_sample_kernel, bb=bb, layer=layer, n_layers=n_layers),
        grid=(b // bb,),
        in_specs=[pl.BlockSpec((bb, MEM_DIM), lambda i: (i, 0)), cspec, cspec],
        out_specs=pl.BlockSpec((bb, MEM_DIM), lambda i: (i, 0)),
        out_shape=jax.ShapeDtypeStruct((b, MEM_DIM), BF16),
        compiler_params=_cparams(1),
        name="mem_attn_sample",
    )(q, cache_k, cache_v)


MLA_DECODE_SEQS = 2


def _mla_decode_kernel(pt_ref, qlat_ref, qr_ref, latn_ref, krn_ref, lat_hbm, kr_hbm, o_ref,
                       lat_buf, kr_buf, sem, m_sc, l_sc, acc_sc, *, layer, n_layers, n_pages):
    n_seq = qlat_ref.shape[0]
    pages, page_rows = lat_buf.shape[1] // n_seq, lat_buf.shape[2]
    n_chunks = pl.num_programs(1)
    c = pl.program_id(1)
    step = pl.program_id(0) * n_chunks + c
    n_steps = pl.num_programs(0) * n_chunks

    def page_copies(page, slot, j):
        lat_cp = pltpu.make_async_copy(lat_hbm.at[pl.ds(pl.multiple_of(page * page_rows, page_rows), page_rows)],
                                       lat_buf.at[slot, j], sem.at[0, slot])
        kr_cp = pltpu.make_async_copy(kr_hbm.at[page, layer], kr_buf.at[slot, j], sem.at[1, slot])
        return lat_cp, kr_cp

    def fetch(at_step, slot):
        group, chunk = at_step // n_chunks, at_step % n_chunks
        for a in range(n_seq):
            base = (group * n_seq + a) * n_pages + chunk * pages
            for j in range(pages):
                for cp in page_copies(pt_ref[base + j], slot, a * pages + j):
                    cp.start()

    @pl.when(step == 0)
    def _():
        fetch(0, 0)

    @pl.when(step + 1 < n_steps)
    def _():
        fetch(step + 1, (step + 1) % 2)

    slot = step % 2
    for j in range(n_seq * pages):
        for cp in page_copies(0, slot, j):
            cp.wait()

    def page_latent(j):
        stride = 2 * n_layers
        lo = lat_buf[slot, j, pl.ds(2 * layer, PAGE_SIZE, stride=stride), :]
        hi = lat_buf[slot, j, pl.ds(2 * layer + 1, PAGE_SIZE, stride=stride), :]
        return jnp.concatenate([lo, hi], axis=-1).astype(BF16)

    @pl.when(c == 0)
    def _():
        for a in range(n_seq):
            latn = latn_ref[a]
            s0 = (jnp.sum(qlat_ref[a].astype(F32) * latn.astype(BF16).astype(F32), -1, keepdims=True)
                  + jnp.sum(qr_ref[a].astype(F32) * krn_ref[a].astype(BF16).astype(F32), -1, keepdims=True)) * MLA_SCALE
            m_sc[a] = s0
            l_sc[a] = jnp.ones_like(s0)
            acc_sc[a] = jnp.broadcast_to(latn.astype(BF16).astype(F32), acc_sc.shape[1:])

    seqs = range(n_seq)
    lats = [[page_latent(a * pages + j) for j in range(pages)] for a in seqs]
    kr_all = [jnp.concatenate([kr_buf[slot, a * pages + j].astype(BF16) for j in range(pages)], axis=1) for a in seqs]
    s_all = [(_dot_t(qlat_ref[a], jnp.concatenate(lats[a], axis=0)) + _dot(qr_ref[a][:, :ROPE_DIM], kr_all[a])) * MLA_SCALE
             for a in seqs]
    scores = [[s_all[a][:, j * PAGE_SIZE:(j + 1) * PAGE_SIZE] for j in range(pages)] for a in seqs]
    row_max = [[jnp.max(sc, -1, keepdims=True) for sc in scores[a]] for a in seqs]
    ms = [[m_sc[a]] for a in seqs]
    for j in range(pages):
        for a in seqs:
            ms[a].append(jnp.maximum(ms[a][-1], row_max[a][j]))
    ps = [[jnp.exp(scores[a][j] - ms[a][j + 1]) for j in range(pages)] for a in seqs]
    corrs = [[jnp.exp(ms[a][j] - ms[a][j + 1]) for j in range(pages)] for a in seqs]
    row_sum = [[jnp.sum(p, -1, keepdims=True) for p in ps[a]] for a in seqs]
    pvs = [[None] * pages for a in seqs]
    for j in range(pages):
        for a in seqs:
            pvs[a][j] = _dot(ps[a][j].astype(BF16), lats[a][j])
    for a in seqs:
        l, acc = l_sc[a], acc_sc[a]
        for j in range(pages):
            l = l * corrs[a][j] + row_sum[a][j]
            acc = acc * corrs[a][j] + pvs[a][j]
        m_sc[a] = ms[a][-1]
        l_sc[a] = l
        acc_sc[a] = acc

    @pl.when(c == n_chunks - 1)
    def _():
        for a in range(n_seq):
            o_ref[a] = acc_sc[a] / l_sc[a]


def _mla_decode(qlat, qr, lat_new, kr_new, cache_lat, cache_kr, page_table, layer):
    b, hp, _ = qlat.shape
    n_pages = page_table.shape[1]
    n_phys, _, n_layers, _ = cache_lat.shape
    pages = min(MLA_PAGES_PER_STEP, n_pages)
    n_seq = MLA_DECODE_SEQS
    assert n_pages % pages == 0 and b % n_seq == 0
    per_b = lambda i, c, pt: (i, 0, 0)
    page_rows = PAGE_SIZE * n_layers * (KV_LORA // LANES)
    cache_lat = cache_lat.reshape(n_phys * page_rows, LANES)
    cache_kr = jnp.transpose(cache_kr, (0, 2, 3, 1))
    any_spec = pl.BlockSpec(memory_space=pl.ANY)
    grid_spec = pltpu.PrefetchScalarGridSpec(
        num_scalar_prefetch=1,
        grid=(b // n_seq, n_pages // pages),
        in_specs=[pl.BlockSpec((n_seq, hp, KV_LORA), per_b), pl.BlockSpec((n_seq, hp, LANES), per_b),
                  pl.BlockSpec((n_seq, 1, KV_LORA), per_b), pl.BlockSpec((n_seq, 1, LANES), per_b), any_spec, any_spec],
        out_specs=pl.BlockSpec((n_seq, hp, KV_LORA), per_b),
        scratch_shapes=[pltpu.VMEM((2, n_seq * pages, page_rows, LANES), F32),
                        pltpu.VMEM((2, n_seq * pages, ROPE_DIM, PAGE_SIZE), F32),
                        pltpu.SemaphoreType.DMA((2, 2)),
                        pltpu.VMEM((n_seq, hp, 1), F32), pltpu.VMEM((n_seq, hp, 1), F32),
                        pltpu.VMEM((n_seq, hp, KV_LORA), F32)],
    )
    return pl.pallas_call(
        functools.partial(_mla_decode_kernel, layer=layer, n_layers=n_layers, n_pages=n_pages),
        grid_spec=grid_spec,
        out_shape=jax.ShapeDtypeStruct((b, hp, KV_LORA), F32),
        compiler_params=_cparams(2),
        name="mla_decode",
    )(page_table.reshape(-1), qlat, qr, lat_new, kr_new, cache_lat, cache_kr)


def _proj_rope_kernel(x_ref, w_ref, c_ref, slo_ref, shi_ref, *out_refs, segs):
    xb = x_ref[...].astype(BF16)
    for c0, n, rope, oi, o0 in segs:
        r = _dot(xb, w_ref[:, c0:c0 + n])
        if rope:
            c, slo, shi = c_ref[...], slo_ref[...], shi_ref[...]
            r = jnp.concatenate([_rope_apply(r[:, a:a + HEAD_DIM], c, slo, shi, HEAD_DIM // 2)
                                 for a in range(0, n, HEAD_DIM)], axis=-1)
        out_refs[oi][:, o0:o0 + n] = r.astype(out_refs[oi].dtype)


def _proj_rope(x, w, tabs, n_tab_blocks, segs, outs):
    m, d = x.shape
    tm = _row_tile(m)
    row = lambda i: (i, 0)
    tab = lambda i: (i % n_tab_blocks, 0)
    return pl.pallas_call(
        functools.partial(_proj_rope_kernel, segs=segs),
        grid=(m // tm,),
        in_specs=[pl.BlockSpec((tm, d), row), pl.BlockSpec(w.shape, lambda i: (0, 0)),
                  pl.BlockSpec((tm, LANES), tab), pl.BlockSpec((tm, LANES), tab), pl.BlockSpec((tm, LANES), tab)],
        out_specs=[pl.BlockSpec((tm, n), row) for n, _ in outs],
        out_shape=[jax.ShapeDtypeStruct((m, n), dt) for n, dt in outs],
        compiler_params=_cparams(1),
        name="proj_rope",
    )(x, w, *tabs)


def _band_attn_kernel(q_ref, kp_ref, kc_ref, vp_ref, vc_ref, o_ref, lse_ref):
    i = pl.program_id(1)
    t = q_ref.shape[1]
    rows = lax.broadcasted_iota(jnp.int32, (t, t), 0)
    cols = lax.broadcasted_iota(jnp.int32, (t, t), 1)
    mask_cur = cols <= rows
    mask_prev = jnp.logical_and(cols >= rows, i > 0)
    for h in range(DIL_KV_HEADS):
        sl = slice(h * HEAD_DIM, (h + 1) * HEAD_DIM)
        q = q_ref[0, :, sl]
        s_c = jnp.where(mask_cur, _dot_t(q, kc_ref[0, :, sl]) * HEAD_SCALE, NEG_INF)
        s_p = jnp.where(mask_prev, _dot_t(q, kp_ref[0, :, sl]) * HEAD_SCALE, NEG_INF)
        m = jnp.maximum(jnp.max(s_c, -1, keepdims=True), jnp.max(s_p, -1, keepdims=True))
        e_c = jnp.exp(s_c - m)
        e_p = jnp.exp(s_p - m)
        l = jnp.sum(e_c, -1, keepdims=True) + jnp.sum(e_p, -1, keepdims=True)
        o = _dot(e_c.astype(BF16), vc_ref[0, :, sl]) + _dot(e_p.astype(BF16), vp_ref[0, :, sl])
        o_ref[0, :, sl] = o / l
        lse_ref[0, :, sl] = jnp.broadcast_to(m + jnp.log(l), (t, HEAD_DIM))


def _band_attn(q, k, v):
    s, l, w = q.shape
    t = DIL_KEYS
    cur = lambda a, i: (a, i, 0)
    prev = lambda a, i: (a, jnp.maximum(i - 1, 0), 0)
    blk = (1, t, w)
    return pl.pallas_call(
        _band_attn_kernel,
        grid=(s, l // t),
        in_specs=[pl.BlockSpec(blk, cur), pl.BlockSpec(blk, prev), pl.BlockSpec(blk, cur),
                  pl.BlockSpec(blk, prev), pl.BlockSpec(blk, cur)],
        out_specs=[pl.BlockSpec(blk, cur), pl.BlockSpec(blk, cur)],
        out_shape=[jax.ShapeDtypeStruct((s, l, w), F32), jax.ShapeDtypeStruct((s, l, w), F32)],
        compiler_params=_cparams(2),
        name="band_attn",
    )(q, k, k, v, v)


def _dil_sample_kernel(q_ref, kn_ref, vn_ref, *refs, bb):
    k_refs, v_refs, o_ref = refs[:N_DIL_GROUPS], refs[N_DIL_GROUPS:2 * N_DIL_GROUPS], refs[2 * N_DIL_GROUPS]
    for j in range(bb):
        outs, lses = [], []
        for g in range(N_DIL_GROUPS):
            q_row = q_ref[pl.ds(j, 1), g * DIL_KV_DIM:(g + 1) * DIL_KV_DIM]
            k_heads, v_heads = [], []
            for h in range(DIL_KV_HEADS):
                sl = slice(h * HEAD_DIM, (h + 1) * HEAD_DIM)
                k_new = jnp.broadcast_to(kn_ref[pl.ds(j, 1), sl], (8, HEAD_DIM))
                v_new = jnp.broadcast_to(vn_ref[pl.ds(j, 1), sl], (8, HEAD_DIM))
                k_heads.append(jnp.concatenate([k_refs[g][j, :, h, :], k_new], 0).astype(BF16))
                v_heads.append(jnp.concatenate([v_refs[g][j, :, h, :], v_new], 0).astype(BF16))
            o, lse = _single_query_attend(q_row, k_heads, v_heads, DIL_KEYS + 1, False)
            outs.append(o)
            lses.append(lse)
        mx = functools.reduce(jnp.maximum, lses)
        ws = [jnp.exp(x - mx) for x in lses]
        mix = sum(o * w for o, w in zip(outs, ws)) / sum(ws)
        o_ref[pl.ds(j, 1), :] = _collapse_heads(mix).astype(o_ref.dtype)


def _dil_sample(q, k_new, v_new, cache_k, cache_v):
    b, wbuf = cache_k.shape[:2]
    bb = 8
    views, specs = [], []
    for window, d in DIL_PATTERNS:
        assert wbuf % (d * DIL_KEYS) == 0 and wbuf >= window
        last = wbuf // d // DIL_KEYS - 1
        specs.append(pl.BlockSpec((bb, DIL_KEYS, None, DIL_KV_HEADS, HEAD_DIM),
                                  functools.partial(lambda i, last_: (i, last_, 0, 0, 0), last_=last)))
        views.append(lambda c, d_=d: c.reshape(b, wbuf // d_, d_, DIL_KV_HEADS, HEAD_DIM))
    row = lambda i: (i, 0)
    return pl.pallas_call(
        functools.partial(_dil_sample_kernel, bb=bb),
        grid=(b // bb,),
        in_specs=[pl.BlockSpec((bb, MIX_HEADS * HEAD_DIM), row), pl.BlockSpec((bb, DIL_KV_DIM), row),
                  pl.BlockSpec((bb, DIL_KV_DIM), row)] + specs + specs,
        out_specs=pl.BlockSpec((bb, DIL_KV_DIM), row),
        out_shape=jax.ShapeDtypeStruct((b, DIL_KV_DIM), BF16),
        compiler_params=_cparams(1),
        name="dil_sample",
    )(q, k_new, v_new, *[f(cache_k) for f in views], *[f(cache_v) for f in views])


ROUTE_GROUP_COL = 0
ROUTE_EXPERT_COL = 8


def _layer_norm(s, g, b):
    mu = jnp.mean(s, -1, keepdims=True)
    d = s - mu
    var = jnp.mean(d * d, -1, keepdims=True)
    return d * lax.rsqrt(var + LN_EPS) * g + b


def _route(x, w, bias):
    logits = _dot(x.astype(BF16), w) + bias
    col = lax.broadcasted_iota(jnp.int32, logits.shape, 1).astype(F32)
    big = float(LANES)
    is_g = col < N_EXPERT_GROUPS
    gl = jnp.where(is_g, logits, NEG_INF)
    gmax = jnp.max(gl, -1, keepdims=True)
    g_sel = jnp.min(jnp.where(gl == gmax, col, big), -1, keepdims=True)
    g_w = 1.0 / jnp.sum(jnp.where(is_g, jnp.exp(gl - gmax), 0.0), -1, keepdims=True)
    lo = ROUTE_EXPERT_COL + EXPERTS_PER_GROUP * g_sel
    el = jnp.where(jnp.logical_and(col >= lo, col < lo + EXPERTS_PER_GROUP), logits, NEG_INF)
    m1 = jnp.max(el, -1, keepdims=True)
    i1 = jnp.min(jnp.where(el == m1, col, big), -1, keepdims=True)
    el2 = jnp.where(col == i1, NEG_INF, el)
    m2 = jnp.max(el2, -1, keepdims=True)
    i2 = jnp.min(jnp.where(el2 == m2, col, big), -1, keepdims=True)
    e2 = jnp.exp(m2 - m1)
    w1 = g_w / (1.0 + e2)
    w2 = g_w * e2 / (1.0 + e2)
    out = jnp.where(col == 0, i1 - ROUTE_EXPERT_COL, 0.0)
    out = jnp.where(col == 1, i2 - ROUTE_EXPERT_COL, out)
    out = jnp.where(col == 2, w1, out)
    return jnp.where(col == 3, w2, out)


def _attn_out_tail(delta, h_ref, g_ref, b_ref, wr_ref, rb_ref, h1_ref, rows_ref, route_ref):
    h1 = _layer_norm(DEEPNORM_ALPHA * h_ref[...] + delta, g_ref[...], b_ref[...])
    h1_ref[...] = h1
    _store_token_rows(rows_ref, h1)
    route_ref[...] = _route(h1, wr_ref[...], rb_ref[...])


def _attn_out_mla_kernel(mix_ref, mem_ref, w_ref, *tail):
    n_mix = mix_ref.shape[1]
    delta = _dot(mix_ref[...], w_ref[0:n_mix, :]) + _dot(mem_ref[...], w_ref[n_mix:, :])
    _attn_out_tail(delta, *tail)


def _attn_out_dil_kernel(o0_ref, o1_ref, o2_ref, l0_ref, l1_ref, l2_ref, mem_ref, w_ref, *tail):
    l0, l1, l2 = l0_ref[...], l1_ref[...], l2_ref[...]
    mx = jnp.maximum(jnp.maximum(l0, l1), l2)
    e0, e1, e2 = jnp.exp(l0 - mx), jnp.exp(l1 - mx), jnp.exp(l2 - mx)
    mix = (o0_ref[...] * e0 + o1_ref[...] * e1 + o2_ref[...] * e2) / (e0 + e1 + e2)
    n_mix = mix.shape[1]
    delta = _dot(mix.astype(BF16), w_ref[0:n_mix, :]) + _dot(mem_ref[...], w_ref[n_mix:, :])
    _attn_out_tail(delta, *tail)


def _attn_out(kernel_fn, parts, w, h, ln_g, ln_b, w_route, r_bias, name):
    m = h.shape[0]
    tm = min(_row_tile(m), 256)
    row = lambda i: (i, 0)
    fixed = lambda i: (0, 0)
    full = lambda a: pl.BlockSpec(a.shape, fixed)
    return pl.pallas_call(
        kernel_fn,
        grid=(m // tm,),
        in_specs=[pl.BlockSpec((tm, p.shape[1]), row) for p in parts]
        + [full(w), pl.BlockSpec((tm, D_MODEL), row), full(ln_g), full(ln_b), full(w_route), full(r_bias)],
        out_specs=[pl.BlockSpec((tm, D_MODEL), row), pl.BlockSpec((tm * ROW_CHUNKS, LANES), row),
                   pl.BlockSpec((tm, LANES), row)],
        out_shape=[jax.ShapeDtypeStruct((m, D_MODEL), F32), jax.ShapeDtypeStruct((m * ROW_CHUNKS, LANES), F32),
                   jax.ShapeDtypeStruct((m, LANES), F32)],
        compiler_params=_cparams(1),
        name=name,
    )(*parts, w, h, ln_g, ln_b, w_route, r_bias)


def _moe_num_tiles(n_tokens):
    worst_rows = 2 * n_tokens + N_EXPERTS * (MOE_TM - 1)
    return -(-worst_rows // MOE_TM)


def _moe_dispatch(route):
    n = route.shape[0]
    ids = route[:, 0:2].astype(jnp.int32).reshape(-1)
    gates = route[:, 2:4].reshape(-1)
    onehot = (ids[:, None] == jnp.arange(N_EXPERTS, dtype=jnp.int32)[None, :]).astype(jnp.int32)
    counts = jnp.sum(onehot, 0)
    rank = jnp.sum((jnp.cumsum(onehot, 0) - onehot) * onehot, 1)
    padded = (counts + MOE_TM - 1) // MOE_TM * MOE_TM
    ends = jnp.cumsum(padded)
    pos = (ends - padded)[ids] + rank
    n_tiles = _moe_num_tiles(n)
    rows = n_tiles * MOE_TM
    packed = jnp.stack([jnp.arange(2 * n, dtype=jnp.int32) // 2, lax.bitcast_convert_type(gates, jnp.int32)], axis=1)
    packed = jnp.zeros((rows, 2), jnp.int32).at[pos].set(packed)
    row_token, row_gate = packed[:, 0], lax.bitcast_convert_type(packed[:, 1], F32)
    tile_start = jnp.arange(n_tiles, dtype=jnp.int32) * MOE_TM
    tile_expert = jnp.minimum(jnp.searchsorted(ends, tile_start, side="right"), N_EXPERTS - 1).astype(jnp.int32)
    n_valid = (ends[-1] // MOE_TM).astype(jnp.int32).reshape(1)
    return tile_expert, n_valid, row_token, row_gate.reshape(rows, 1), pos.astype(jnp.int32)


def _row_copy(src_hbm, src_row, dst_buf, dst_row, sem):
    src = pl.ds(pl.multiple_of(src_row * ROW_CHUNKS, ROW_CHUNKS), ROW_CHUNKS)
    dst = pl.ds(pl.multiple_of(dst_row * ROW_CHUNKS, ROW_CHUNKS), ROW_CHUNKS)
    return pltpu.make_async_copy(src_hbm.at[src], dst_buf.at[dst], sem)


DMA_LOOP_UNROLL = 8


def _moe_ffn_kernel(te_ref, nv_ref, rt_ref, x_hbm, gate_ref, wg_ref, wu_ref, wd_ref, y_ref,
                    xbuf, sem, wg_b, wu_b, wd_b):
    i = pl.program_id(0)
    n_valid = nv_ref[0]
    tm = xbuf.shape[1] // ROW_CHUNKS

    def gather(tile, slot):
        def body(k, carry):
            _row_copy(x_hbm, rt_ref[tile * tm + k], xbuf.at[slot], k, sem.at[slot]).start()
            return carry
        lax.fori_loop(0, tm, body, 0, unroll=DMA_LOOP_UNROLL)

    def gather_wait(slot):
        def body(k, carry):
            _row_copy(x_hbm, 0, xbuf.at[slot], k, sem.at[slot]).wait()
            return carry
        lax.fori_loop(0, tm, body, 0, unroll=DMA_LOOP_UNROLL)

    @pl.when(jnp.logical_and(i == 0, n_valid > 0))
    def _():
        gather(0, 0)

    @pl.when(i + 1 < n_valid)
    def _():
        gather(i + 1, (i + 1) % 2)

    @pl.when(i < n_valid)
    def _():
        new_expert = jnp.logical_or(i == 0, te_ref[i] != te_ref[jnp.maximum(i - 1, 0)])

        @pl.when(new_expert)
        def _():
            wg_b[...] = wg_ref[0].astype(BF16)
            wu_b[...] = wu_ref[0].astype(BF16)
            wd_b[...] = wd_ref[0].astype(BF16)

        slot = i % 2
        gather_wait(slot)
        xb = _load_token_rows(xbuf.at[slot], tm).astype(BF16)
        g = _dot(xb, wg_b[...])
        u = _dot(xb, wu_b[...])
        hid = g / (1.0 + jnp.exp(-g)) * u * gate_ref[...]
        _store_token_rows(y_ref, _dot(hid.astype(BF16), wd_b[...]))

    @pl.when(i >= n_valid)
    def _():
        y_ref[...] = jnp.zeros_like(y_ref)


def _moe_ffn(x, w_g, w_u, w_d, layer, tile_expert, n_valid, row_token, row_gate):
    d = w_g.shape[2]
    hdim = w_g.shape[3]
    rows = row_token.shape[0]
    tm = MOE_TM
    n_tiles = rows // tm
    live = lambda i, te, nv, rt: jnp.minimum(i, jnp.maximum(nv[0] - 1, 0))
    grid_spec = pltpu.PrefetchScalarGridSpec(
        num_scalar_prefetch=3,
        grid=(n_tiles,),
        in_specs=[pl.BlockSpec(memory_space=pl.ANY),
                  pl.BlockSpec((tm, 1), lambda i, te, nv, rt: (live(i, te, nv, rt), 0)),
                  pl.BlockSpec((None, 1, d, hdim), lambda i, te, nv, rt: (layer, te[live(i, te, nv, rt)], 0, 0)),
                  pl.BlockSpec((None, 1, d, hdim), lambda i, te, nv, rt: (layer, te[live(i, te, nv, rt)], 0, 0)),
                  pl.BlockSpec((None, 1, hdim, d), lambda i, te, nv, rt: (layer, te[live(i, te, nv, rt)], 0, 0))],
        out_specs=pl.BlockSpec((tm * ROW_CHUNKS, LANES), lambda i, te, nv, rt: (i, 0)),
        scratch_shapes=[pltpu.VMEM((2, tm * ROW_CHUNKS, LANES), F32), pltpu.SemaphoreType.DMA((2,)),
                        pltpu.VMEM((d, hdim), BF16), pltpu.VMEM((d, hdim), BF16), pltpu.VMEM((hdim, d), BF16)],
    )
    return pl.pallas_call(
        _moe_ffn_kernel,
        grid_spec=grid_spec,
        out_shape=jax.ShapeDtypeStruct((rows * ROW_CHUNKS, LANES), F32),
        compiler_params=_cparams(1),
        name="moe_ffn",
    )(tile_expert, n_valid, row_token, x, row_gate, w_g, w_u, w_d)


def _moe_combine_kernel(pos_ref, h_ref, y_hbm, g_ref, b_ref, o_ref, ybuf, sem, *, tok0):
    i = pl.program_id(0)
    n_steps = pl.num_programs(0)
    tm = h_ref.shape[0]

    def gather(step, slot):
        def body(k, carry):
            base = 2 * (tok0 + step * tm + k)
            _row_copy(y_hbm, pos_ref[base], ybuf.at[slot, 0], k, sem.at[slot]).start()
            _row_copy(y_hbm, pos_ref[base + 1], ybuf.at[slot, 1], k, sem.at[slot]).start()
            return carry
        lax.fori_loop(0, tm, body, 0, unroll=DMA_LOOP_UNROLL)

    def gather_wait(slot):
        def body(k, carry):
            _row_copy(y_hbm, 0, ybuf.at[slot, 0], k, sem.at[slot]).wait()
            _row_copy(y_hbm, 0, ybuf.at[slot, 1], k, sem.at[slot]).wait()
            return carry
        lax.fori_loop(0, tm, body, 0, unroll=DMA_LOOP_UNROLL)

    @pl.when(i == 0)
    def _():
        gather(0, 0)

    @pl.when(i + 1 < n_steps)
    def _():
        gather(i + 1, (i + 1) % 2)

    slot = i % 2
    gather_wait(slot)
    s = DEEPNORM_ALPHA * h_ref[...] + (_load_token_rows(ybuf.at[slot, 0], tm) + _load_token_rows(ybuf.at[slot, 1], tm))
    o_ref[...] = _layer_norm(s, g_ref[...], b_ref[...])


def _moe_combine(pos, h1, y_sorted, ln_g, ln_b, tok0):
    n_rows, d = h1.shape
    tm = min(_row_tile(n_rows), 256)
    grid_spec = pltpu.PrefetchScalarGridSpec(
        num_scalar_prefetch=1,
        grid=(n_rows // tm,),
        in_specs=[pl.BlockSpec((tm, d), lambda i, p: (i, 0)),
                  pl.BlockSpec(memory_space=pl.ANY),
                  pl.BlockSpec((1, d), lambda i, p: (0, 0)), pl.BlockSpec((1, d), lambda i, p: (0, 0))],
        out_specs=pl.BlockSpec((tm, d), lambda i, p: (i, 0)),
        scratch_shapes=[pltpu.VMEM((2, 2, tm * ROW_CHUNKS, LANES), F32), pltpu.SemaphoreType.DMA((2,))],
    )
    return pl.pallas_call(
        functools.partial(_moe_combine_kernel, tok0=tok0),
        grid_spec=grid_spec,
        out_shape=jax.ShapeDtypeStruct((n_rows, d), F32),
        compiler_params=_cparams(1),
        name="moe_combine",
    )(pos, h1, y_sorted, ln_g, ln_b)


def _router_weights(w_rg, b_rg, w_re, b_re):
    d = w_rg.shape[0]
    w = jnp.zeros((d, LANES), F32)
    w = w.at[:, ROUTE_GROUP_COL:ROUTE_GROUP_COL + N_EXPERT_GROUPS].set(w_rg)
    w = w.at[:, ROUTE_EXPERT_COL:ROUTE_EXPERT_COL + N_EXPERTS].set(w_re)
    bias = jnp.zeros((1, LANES), F32)
    bias = bias.at[0, ROUTE_GROUP_COL:ROUTE_GROUP_COL + N_EXPERT_GROUPS].set(b_rg)
    bias = bias.at[0, ROUTE_EXPERT_COL:ROUTE_EXPERT_COL + N_EXPERTS].set(b_re)
    return w.astype(BF16), bias


def _moe_block(prompt, sample, w_g, w_u, w_d, layer, ln_g, ln_b):
    (h1_p, rows_p, route_p), (h1_s, rows_s, route_s) = prompt, sample
    rows = jnp.concatenate([rows_p, rows_s], axis=0)
    route = jnp.concatenate([route_p, route_s], axis=0)
    tile_expert, n_valid, row_token, row_gate, pos = _moe_dispatch(route)
    y_sorted = _moe_ffn(rows, w_g, w_u, w_d, layer, tile_expert, n_valid, row_token, row_gate)
    out_p = _moe_combine(pos, h1_p, y_sorted, ln_g, ln_b, 0)
    out_s = _moe_combine(pos, h1_s, y_sorted, ln_g, ln_b, h1_p.shape[0])
    return out_p, out_s


def kernel(x_prompt, x_sample, cache_mla_latent, cache_mla_krope, cache_swa_k, cache_swa_v, cache_mem_k, cache_mem_v, page_table, mem_prompt, w_in_a, q_norm_a, kv_norm_a, w_q_up_a, w_kv_up_a, w_out_a, w_in_b, w_kv_shared, w_out_b, w_mem_kv, ln1_g, ln1_b, ln2_g, ln2_b, w_router_group, b_router_group, w_router_expert, b_router_expert, w_exp_gate, w_exp_up, w_exp_down):
    bp, tp, d = x_prompt.shape
    bs, ts, _ = x_sample.shape
    assert ts == 1 and w_in_a.shape[0] == 1 and w_in_b.shape[0] == 1
    n_p, n_s = bp * tp, bs * ts
    n_mem = mem_prompt.shape[1]
    wbuf = cache_swa_k.shape[1]
    past_len = page_table.shape[1] * PAGE_SIZE
    assert wbuf == DIL_PATTERNS[-1][0]

    pos_p = jnp.arange(tp, dtype=jnp.int32)
    pos_s = jnp.full((n_s,), past_len, jnp.int32)
    tm_p = _row_tile(n_p)
    assert tp % tm_p == 0
    tab_blocks_p = tp // tm_p
    tabs_r_p = _rope_tables(pos_p, ROPE_DIM, LANES)
    tabs_r_s = _rope_tables(pos_s, ROPE_DIM, LANES)
    tabs_h_p = _rope_tables(pos_p, HEAD_DIM, LANES)
    tabs_h_s = _rope_tables(pos_s, HEAD_DIM, LANES)

    hp = x_prompt.reshape(n_p, d)
    hs = x_sample.reshape(n_s, d)
    ln = lambda a, l: a[l].reshape(1, d)

    mem2d = mem_prompt.reshape(bp * n_mem, d)
    mkv = [_matmul(mem2d, w_mem_kv[l].astype(BF16)).reshape(bp, n_mem, 2 * MEM_DIM) for l in range(DEPTH)]

    w_in = w_in_a[0]
    o_kr = Q_LORA + KV_LORA
    w_in = jnp.concatenate([w_in[:, :o_kr + ROPE_DIM], jnp.zeros((d, LANES - ROPE_DIM), F32),
                            w_in[:, o_kr + ROPE_DIM:]], axis=1).astype(BF16)
    w_q = jnp.concatenate([w_q_up_a[0], jnp.zeros((Q_LORA, MIX_HEADS, MLA_QK - NOPE_DIM - ROPE_DIM), F32)], axis=-1)
    w_q = w_q.reshape(Q_LORA, MIX_HEADS * MLA_QK).astype(BF16)
    w_kv = w_kv_up_a[0].reshape(KV_LORA, MIX_HEADS * (NOPE_DIM + V_DIM)).astype(BF16)
    w_uk_t = jnp.transpose(w_kv_up_a[0][..., :NOPE_DIM], (1, 2, 0)).astype(BF16)
    w_uv = jnp.transpose(w_kv_up_a[0][..., NOPE_DIM:], (1, 0, 2)).astype(BF16)
    q_g, kv_g = q_norm_a[0].reshape(1, Q_LORA), kv_norm_a[0].reshape(1, KV_LORA)

    cq_p, lat_p, kr_p, qm_p = _mla_in_proj(hp, w_in, q_g, kv_g, tabs_r_p, tab_blocks_p)
    cq_s, lat_s, kr_s, qm_s = _mla_in_proj(hs, w_in, q_g, kv_g, tabs_r_s, 1)
    q_p = _q_up(cq_p, w_q, tabs_r_p, tab_blocks_p)
    q_s = _q_up(cq_s, w_q, tabs_r_s, 1)

    k_full, v_full = _kv_up(lat_p, kr_p, w_kv)
    mix_p, shifted_k, shifted_v = _mla_flash(q_p.reshape(bp, tp, -1), k_full.reshape(bp, tp, -1),
                                             v_full.reshape(bp, tp, -1), cache_swa_k, cache_swa_v)
    mix_p = mix_p.reshape(n_p, -1)

    hpad = 16
    q_s3 = q_s.reshape(n_s, MIX_HEADS, MLA_QK)
    qlat = _head_matmul(jnp.transpose(q_s3[:, :, :NOPE_DIM], (1, 0, 2)), w_uk_t, BF16)
    qlat = jnp.pad(jnp.transpose(qlat, (1, 0, 2)), ((0, 0), (0, hpad - MIX_HEADS), (0, 0)))
    qr = jnp.pad(q_s3[:, :, NOPE_DIM:], ((0, 0), (0, hpad - MIX_HEADS), (0, 0)))
    o_lat = _mla_decode(qlat, qr, lat_s.reshape(n_s, 1, KV_LORA), kr_s.reshape(n_s, 1, LANES),
                        cache_mla_latent, cache_mla_krope, page_table, 0)
    mix_s = _head_matmul(jnp.transpose(o_lat[:, :MIX_HEADS], (1, 0, 2)), w_uv, BF16)
    mix_s = jnp.transpose(mix_s, (1, 0, 2)).reshape(n_s, MIX_HEADS * V_DIM)

    mem_p = _mem_attn_prompt(qm_p.reshape(bp, tp, MEM_DIM), mkv[0]).reshape(n_p, MEM_DIM)
    mem_s = _mem_attn_sample(qm_s, cache_mem_k, cache_mem_v, 0)

    router = [_router_weights(w_router_group[l], b_router_group[l], w_router_expert[l], b_router_expert[l])
              for l in range(DEPTH)]
    w_out = w_out_a[0].astype(BF16)
    out_p = _attn_out(_attn_out_mla_kernel, [mix_p, mem_p], w_out, hp, ln(ln1_g, 0), ln(ln1_b, 0),
                      *router[0], name="attn_out_mla")
    out_s = _attn_out(_attn_out_mla_kernel, [mix_s, mem_s], w_out, hs, ln(ln1_g, 0), ln(ln1_b, 0),
                      *router[0], name="attn_out_mla")
    hp, hs = _moe_block(out_p, out_s, w_exp_gate, w_exp_up, w_exp_down, 0, ln(ln2_g, 0), ln(ln2_b, 0))

    n_q = MIX_HEADS * HEAD_DIM
    w_b = w_in_b[0].astype(BF16)
    w_kvs = w_kv_shared.astype(BF16)
    q_segs = tuple((g * DIL_KV_DIM, DIL_KV_DIM, True, 0, g * DIL_KV_DIM) for g in range(N_DIL_GROUPS))
    q_segs += ((n_q, MEM_DIM, False, 1, 0),)
    kv_segs = ((0, DIL_KV_DIM, True, 0, 0), (DIL_KV_DIM, DIL_KV_DIM, False, 1, 0))
    qd_p, qm_p = _proj_rope(hp, w_b, tabs_h_p, tab_blocks_p, q_segs, [(n_q, BF16), (MEM_DIM, BF16)])
    qd_s, qm_s = _proj_rope(hs, w_b, tabs_h_s, 1, q_segs, [(n_q, F32), (MEM_DIM, BF16)])
    k_p, v_p = _proj_rope(hp, w_kvs, tabs_h_p, tab_blocks_p, kv_segs, [(DIL_KV_DIM, F32), (DIL_KV_DIM, F32)])
    k_s, v_s = _proj_rope(hs, w_kvs, tabs_h_s, 1, kv_segs, [(DIL_KV_DIM, F32), (DIL_KV_DIM, F32)])

    k_pb, v_pb = k_p.astype(BF16), v_p.astype(BF16)
    outs, lses = [], []
    for g, (_, dil) in enumerate(DIL_PATTERNS):
        def split(a):
            a = a.reshape(bp, tp // dil, dil, DIL_KV_DIM)
            return jnp.transpose(a, (0, 2, 1, 3)).reshape(bp * dil, tp // dil, DIL_KV_DIM)

        def merge(a):
            a = a.reshape(bp, dil, tp // dil, DIL_KV_DIM)
            return jnp.transpose(a, (0, 2, 1, 3)).reshape(n_p, DIL_KV_DIM)

        o_g, lse_g = _band_attn(split(qd_p[:, g * DIL_KV_DIM:(g + 1) * DIL_KV_DIM]), split(k_pb), split(v_pb))
        outs.append(merge(o_g))
        lses.append(merge(lse_g))

    mix_s = _dil_sample(qd_s, k_s, v_s, cache_swa_k, cache_swa_v)
    mem_p = _mem_attn_prompt(qm_p.reshape(bp, tp, MEM_DIM), mkv[1]).reshape(n_p, MEM_DIM)
    mem_s = _mem_attn_sample(qm_s, cache_mem_k, cache_mem_v, 1)

    w_out = w_out_b[0].astype(BF16)
    out_p = _attn_out(_attn_out_dil_kernel, outs + lses + [mem_p], w_out, hp, ln(ln1_g, 1), ln(ln1_b, 1),
                      *router[1], name="attn_out_dil")
    out_s = _attn_out(_attn_out_mla_kernel, [mix_s, mem_s], w_out, hs, ln(ln1_g, 1), ln(ln1_b, 1),
                      *router[1], name="attn_out_mla")
    hp, hs = _moe_block(out_p, out_s, w_exp_gate, w_exp_up, w_exp_down, 1, ln(ln2_g, 1), ln(ln2_b, 1))

    kv4 = lambda a, n: a.reshape(n, -1, DIL_KV_HEADS, HEAD_DIM)
    new_k_s = _swa_set_newest(shifted_k, kv4(k_s, bs))
    new_v_s = _swa_set_newest(shifted_v, kv4(v_s, bs))
    keep_p = min(wbuf, tp)
    k_p4, v_p4 = kv4(k_p, bp), kv4(v_p, bp)
    mem_k = jnp.stack([m[:, :, :MEM_DIM].reshape(bp, n_mem, MEM_HEADS, HEAD_DIM) for m in mkv], axis=2)
    mem_v = jnp.stack([m[:, :, MEM_DIM:].reshape(bp, n_mem, MEM_HEADS, HEAD_DIM) for m in mkv], axis=2)
    return (hp.reshape(bp, tp, d), hs.reshape(bs, ts, d),
            lat_p.reshape(bp, tp, 1, KV_LORA), kr_p[:, :ROPE_DIM].reshape(bp, tp, 1, ROPE_DIM),
            lat_s.reshape(bs, ts, 1, KV_LORA), kr_s[:, :ROPE_DIM].reshape(bs, ts, 1, ROPE_DIM),
            k_p4[:, tp - keep_p:], v_p4[:, tp - keep_p:], new_k_s, new_v_s, mem_k, mem_v)
```
